```python
import jax, jax.numpy as jnp
from jax import lax
import numpy as np

D_MODEL = 1024
BATCH = 4
SEQ = 4096
DEPTH = 4
DEC_BATCH = 8
DEC_SEQ = 8192
PAST_LEN = 128

N_MIXERS = 2
N_MLSTM_LAYERS = (DEPTH + 1) // 2
N_FOURIER_LAYERS = DEPTH // 2
M_HEADS = 4
M_DQK = D_MODEL // (2 * M_HEADS)
M_DV = D_MODEL // M_HEADS
M_Q = M_HEADS * M_DQK
M_V = M_HEADS * M_DV
M_IN = 2 * M_Q + 2 * M_V + 4 * M_HEADS
CHUNK = 64
F_GROUPS = 4
F_GROUP_DIM = D_MODEL // F_GROUPS
D_FF = 2816
CONV_W = 3
EPS = 1e-6

kernel_name = "hybrid_mlstm_fnet_convffn_encoder"


def rmsnorm(x, g):
    xf = x.astype(jnp.float32)
    y = xf * lax.rsqrt(jnp.mean(xf * xf, axis=-1, keepdims=True) + EPS)
    return (y * g.astype(jnp.float32)).astype(x.dtype)


def _to_chunks(a):
    b, h, l = a.shape[:3]
    a = a.reshape((b, h, l // CHUNK, CHUNK) + a.shape[3:])
    return jnp.moveaxis(a, 2, 0)


def mlstm_scan(q, k, v, ig, lf):
    b_, h_, l_, _ = q.shape
    init = (jnp.zeros((b_, h_, M_DV, M_DQK), jnp.float32),
            jnp.zeros((b_, h_, M_DQK), jnp.float32),
            jnp.zeros((b_, h_), jnp.float32))
    mask = jnp.tril(jnp.ones((CHUNK, CHUNK), dtype=bool))

    def step(carry, xs):
        c_prev, n_prev, m_prev = carry
        qc, kc, vc, ic, fc = xs
        b = jnp.cumsum(fc, axis=-1)
        dlog = b[..., :, None] - b[..., None, :] + ic[..., None, :]
        dlog = jnp.where(mask, dlog, -jnp.inf)
        inter = b + m_prev[..., None]
        m_t = jnp.maximum(inter, jnp.max(dlog, axis=-1))
        w = jnp.exp(dlog - m_t[..., None])
        s = jnp.einsum('bhtd,bhsd->bhts', qc, kc) * w
        sc_inter = jnp.exp(inter - m_t)
        num = (jnp.einsum('bhts,bhsv->bhtv', s, vc)
               + sc_inter[..., None] * jnp.einsum('bhvd,bhtd->bhtv', c_prev, qc))
        den = jnp.sum(s, axis=-1) + sc_inter * jnp.einsum('bhd,bhtd->bht', n_prev, qc)
        h = num / jnp.maximum(jnp.abs(den), jnp.exp(-m_t))[..., None]
        b_last = b[..., -1]
        g = b_last[..., None] - b + ic
        m_new = jnp.maximum(b_last + m_prev, jnp.max(g, axis=-1))
        wk = jnp.exp(g - m_new[..., None])
        decay = jnp.exp(b_last + m_prev - m_new)
        c_new = decay[..., None, None] * c_prev + jnp.einsum('bhsv,bhsd->bhvd', vc * wk[..., None], kc)
        n_new = decay[..., None] * n_prev + jnp.einsum('bhs,bhsd->bhd', wk, kc)
        return (c_new, n_new, m_new), h

    xs = (_to_chunks(q), _to_chunks(k), _to_chunks(v), _to_chunks(ig), _to_chunks(lf))
    _, hs = lax.scan(step, init, xs)
    return jnp.moveaxis(hs, 0, 2).reshape(b_, h_, l_, M_DV)


def mlstm_mixer(h, w_in, b_gate, head_g, w_out):
    bsz, l, _ = h.shape
    proj = h @ w_in
    q = proj[..., :M_Q]
    k = proj[..., M_Q:2 * M_Q]
    v = proj[..., 2 * M_Q:2 * M_Q + M_V]
    o = proj[..., 2 * M_Q + M_V:2 * M_Q + 2 * M_V]
    gates = (proj[..., 2 * M_Q + 2 * M_V:] + b_gate).astype(jnp.float32)

    def heads(a, d):
        return a.astype(jnp.float32).reshape(bsz, l, M_HEADS, d).transpose(0, 2, 1, 3)

    q = heads(q, M_DQK)
    k = heads(k, M_DQK) * (M_DQK ** -0.5)
    v = heads(v, M_DV)
    gates = gates.reshape(bsz, l, 4, M_HEADS).transpose(2, 0, 3, 1)
    ig_f, fg_f, ig_b, fg_b = gates[0], gates[1], gates[2], gates[3]
    lf_f = jax.nn.log_sigmoid(fg_f)
    lf_b = jax.nn.log_sigmoid(fg_b)

    h_fwd = mlstm_scan(q, k, v, ig_f, lf_f)
    h_bwd = mlstm_scan(q[:, :, ::-1], k[:, :, ::-1], v[:, :, ::-1],
                       ig_b[:, :, ::-1], lf_b[:, :, ::-1])[:, :, ::-1]
    hs = h_fwd + h_bwd
    hs = hs * lax.rsqrt(jnp.mean(hs * hs, axis=-1, keepdims=True) + EPS)
    hs = hs * head_g.astype(jnp.float32)[None, :, None, :]
    hs = hs.transpose(0, 2, 1, 3).reshape(bsz, l, M_V)
    y = hs * jax.nn.sigmoid(o.astype(jnp.float32))
    return y.astype(h.dtype) @ w_out


def fourier_mixer(h, w_out, b_out):
    bsz, l, d = h.shape
    hg = h.astype(jnp.float32).reshape(bsz, l, F_GROUPS, F_GROUP_DIM).transpose(0, 2, 1, 3)
    f = jnp.fft.fft2(hg, norm="ortho").real
    f = f.transpose(0, 2, 1, 3).reshape(bsz, l, d).astype(h.dtype)
    return f @ w_out + b_out


def conv_ffn(h, w_up, conv_w, conv_b, w_down):
    l = h.shape[1]
    u = h @ w_up
    p = CONV_W // 2
    up = jnp.pad(u, ((0, 0), (p, p), (0, 0)))
    y = conv_b
    for j in range(CONV_W):
        y = y + up[:, j:j + l] * conv_w[j]
    a, val = jnp.split(y, 2, axis=-1)
    return (jax.nn.silu(a) * val) @ w_down


def trunk(x, mix_norm_g, m_w_in, m_b_gate, m_head_g, m_w_out, f_w_out, f_b_out,
          ffn_norm_g, ffn_w_up, ffn_conv_w, ffn_conv_b, ffn_w_down, final_norm_g):
    for i in range(DEPTH):
        j = i // N_MIXERS
        h = rmsnorm(x, mix_norm_g[i])
        if i % N_MIXERS == 0:
            x = x + mlstm_mixer(h, m_w_in[j], m_b_gate[j], m_head_g[j], m_w_out[j])
        else:
            x = x + fourier_mixer(h, f_w_out[j], f_b_out[j])
        h = rmsnorm(x, ffn_norm_g[i])
        x = x + conv_ffn(h, ffn_w_up[i], ffn_conv_w[i], ffn_conv_b[i], ffn_w_down[i])
    return rmsnorm(x, final_norm_g)


def setup_inputs(seed: int = 0) -> dict:
    key = jax.random.key(seed)
    ks = jax.random.split(key, 16)
    f32 = jnp.float32
    nrm = lambda k, s: jax.random.normal(k, s, f32)
    lin = jnp.linspace(3.0, 6.0, M_HEADS, dtype=f32)
    zer = jnp.zeros((M_HEADS,), f32)
    gate_base = jnp.concatenate([zer, lin, zer, lin])[None, :]
    return {
        "x_prompt": nrm(ks[0], (BATCH, SEQ, D_MODEL)),
        "x_sample": nrm(ks[1], (DEC_BATCH, DEC_SEQ, D_MODEL)),
        "mix_norm_g": 1.0 + 0.02 * nrm(ks[2], (DEPTH, D_MODEL)),
        "m_w_in": nrm(ks[3], (N_MLSTM_LAYERS, D_MODEL, M_IN)) * D_MODEL ** -0.5,
        "m_b_gate": gate_base + 0.1 * nrm(ks[4], (N_MLSTM_LAYERS, 4 * M_HEADS)),
        "m_head_g": 1.0 + 0.02 * nrm(ks[5], (N_MLSTM_LAYERS, M_HEADS, M_DV)),
        "m_w_out": nrm(ks[6], (N_MLSTM_LAYERS, M_V, D_MODEL)) * M_V ** -0.5,
        "f_w_out": nrm(ks[7], (N_FOURIER_LAYERS, D_MODEL, D_MODEL)) * D_MODEL ** -0.5,
        "f_b_out": 0.01 * nrm(ks[8], (N_FOURIER_LAYERS, D_MODEL)),
        "ffn_norm_g": 1.0 + 0.02 * nrm(ks[9], (DEPTH, D_MODEL)),
        "ffn_w_up": nrm(ks[10], (DEPTH, D_MODEL, 2 * D_FF)) * D_MODEL ** -0.5,
        "ffn_conv_w": nrm(ks[11], (DEPTH, CONV_W, 2 * D_FF)) * CONV_W ** -0.5,
        "ffn_conv_b": 0.01 * nrm(ks[12], (DEPTH, 2 * D_FF)),
        "ffn_w_down": nrm(ks[13], (DEPTH, D_FF, D_MODEL)) * D_FF ** -0.5,
        "final_norm_g": 1.0 + 0.02 * nrm(ks[14], (D_MODEL,)),
    }


def reference(x_prompt, x_sample, mix_norm_g, m_w_in, m_b_gate, m_head_g, m_w_out,
              f_w_out, f_b_out, ffn_norm_g, ffn_w_up, ffn_conv_w, ffn_conv_b,
              ffn_w_down, final_norm_g):
    y_prompt = trunk(x_prompt, mix_norm_g, m_w_in, m_b_gate, m_head_g, m_w_out, f_w_out, f_b_out,
                     ffn_norm_g, ffn_w_up, ffn_conv_w, ffn_conv_b, ffn_w_down, final_norm_g)
    y_sample = trunk(x_sample, mix_norm_g, m_w_in, m_b_gate, m_head_g, m_w_out, f_w_out, f_b_out,
                     ffn_norm_g, ffn_w_up, ffn_conv_w, ffn_conv_b, ffn_w_down, final_norm_g)
    return (y_prompt, y_sample)
```

```python
import functools

import numpy as np
import jax
import jax.numpy as jnp
from jax import lax
from jax.experimental import pallas as pl
from jax.experimental.pallas import tpu as pltpu

F32 = jnp.float32
BF16 = jnp.bfloat16

EPS = 1e-6
F_GROUPS = 4
N_GATE_KINDS = 4

V7X_VMEM_BYTES = 64 * 1024 * 1024
V7X_LANES = 128
V7X_SUBLANES = 8
V7X_MXU_DIM = 256

HALO = V7X_SUBLANES
SCAN_CHUNK = 128
DFT_INNER = 128


def _vmem_limit(nbytes):
    return int(min(max(nbytes, 32 * 1024 * 1024), V7X_VMEM_BYTES - 8 * 1024 * 1024))


def _rmsnorm(x, g):
    y = x * lax.rsqrt(jnp.mean(x * x, axis=-1, keepdims=True) + EPS)
    return y * g


def _resident(shape):
    nd = len(shape)
    return pl.BlockSpec(shape, lambda *_: (0,) * nd, pipeline_mode=pl.Buffered(1))


def _ffn_kernel(xm_ref, xp_ref, xn_ref, g_ref, wup_ref, cw_ref, cb_ref, wdn_ref, fg_ref,
                o_ref, h_ref, act_ref, *, tm, n_chunks, fc, conv_w, final_norm):
    i = pl.program_id(1)
    last = pl.num_programs(1) - 1
    g = g_ref[...]
    x = xm_ref[0]
    hp = jnp.where(i > 0, _rmsnorm(xp_ref[0], g), 0.0)
    hn = jnp.where(i < last, _rmsnorm(xn_ref[0], g), 0.0)
    h_ref[0:HALO, :] = hp.astype(BF16)
    h_ref[HALO:HALO + tm, :] = _rmsnorm(x, g).astype(BF16)
    h_ref[HALO + tm:, :] = hn.astype(BF16)
    hb = h_ref[...]
    pad = conv_w // 2
    for c in range(n_chunks):
        u = jnp.dot(hb, wup_ref[c], preferred_element_type=F32)
        w = cw_ref[c]
        y = cb_ref[c]
        for j in range(conv_w):
            off = HALO - pad + j
            y = y + u[off:off + tm] * w[j:j + 1]
        a = y[:, :fc]
        val = y[:, fc:]
        act_ref[:, c * fc:(c + 1) * fc] = (a * jax.nn.sigmoid(a) * val).astype(BF16)
    r = x + jnp.dot(act_ref[...], wdn_ref[...], preferred_element_type=F32)
    if final_norm:
        r = _rmsnorm(r, fg_ref[...])
    o_ref[0] = r


def _conv_ffn(x, norm_g, w_up, conv_w, conv_b, w_down, final_g, *, final_norm):
    b, l, d = x.shape
    d_ff = w_down.shape[0]
    cw = conv_w.shape[0]
    fc = V7X_MXU_DIM
    n_chunks = d_ff // fc
    assert n_chunks * fc == d_ff and cw // 2 <= HALO
    tm = min(512, l)
    assert l % tm == 0 and tm % HALO == 0
    nt = l // tm
    rb = tm // HALO
    wup = w_up.reshape(d, 2, n_chunks, fc).transpose(2, 0, 1, 3).reshape(n_chunks, d, 2 * fc).astype(BF16)
    cwr = conv_w.reshape(cw, 2, n_chunks, fc).transpose(2, 0, 1, 3).reshape(n_chunks, cw, 2 * fc)
    cbr = conv_b.reshape(2, n_chunks, fc).transpose(1, 0, 2).reshape(n_chunks, 1, 2 * fc)
    wdn = w_down.astype(BF16)
    kern = functools.partial(_ffn_kernel, tm=tm, n_chunks=n_chunks, fc=fc, conv_w=cw, final_norm=final_norm)
    vmem = (wup.size * 2 + wdn.size * 2 + 4 * tm * d * 4 + (tm + 2 * HALO) * d * 2 + tm * d_ff * 2
            + 6 * (tm + 2 * HALO) * 2 * fc * 4 + 4 * tm * d * 4)
    return pl.pallas_call(
        kern,
        grid=(b, nt),
        in_specs=[
            pl.BlockSpec((1, tm, d), lambda bi, i: (bi, i, 0)),
            pl.BlockSpec((1, HALO, d), lambda bi, i: (bi, jnp.maximum(i * rb - 1, 0), 0)),
            pl.BlockSpec((1, HALO, d), lambda bi, i: (bi, jnp.minimum((i + 1) * rb, l // HALO - 1), 0)),
            _resident((1, d)),
            _resident(wup.shape),
            _resident(cwr.shape),
            _resident(cbr.shape),
            _resident(wdn.shape),
            _resident((1, d)),
        ],
        out_specs=pl.BlockSpec((1, tm, d), lambda bi, i: (bi, i, 0)),
        out_shape=jax.ShapeDtypeStruct((b, l, d), F32),
        scratch_shapes=[pltpu.VMEM((tm + 2 * HALO, d), BF16), pltpu.VMEM((tm, d_ff), BF16)],
        compiler_params=pltpu.CompilerParams(
            dimension_semantics=("parallel", "arbitrary"), vmem_limit_bytes=_vmem_limit(vmem)),
        name="conv_ffn",
    )(x, x, x, norm_g.reshape(1, d), wup, cwr, cbr, wdn, final_g.reshape(1, d))


def _proj_kernel(x_ref, g_ref, w_ref, wg_ref, bg_ref, q_ref, k_ref, v_ref, o_ref, gc_ref, gr_ref,
                 *, mq, mv, heads, k_scale):
    h = _rmsnorm(x_ref[0], g_ref[...]).astype(BF16)
    q_ref[0] = jnp.dot(h, w_ref[:, 0:mq], preferred_element_type=F32).astype(BF16)
    k_ref[0] = (jnp.dot(h, w_ref[:, mq:2 * mq], preferred_element_type=F32) * k_scale).astype(BF16)
    v_ref[0] = jnp.dot(h, w_ref[:, 2 * mq:2 * mq + mv], preferred_element_type=F32).astype(BF16)
    o_ref[0] = jnp.dot(h, w_ref[:, 2 * mq + mv:], preferred_element_type=F32).astype(BF16)
    gt = jnp.dot(h, wg_ref[...], preferred_element_type=F32) + bg_ref[...]
    lane = lax.broadcasted_iota(jnp.int32, gt.shape, 1)
    is_forget = ((lane // heads) % 2 == 1) & (lane < N_GATE_KINDS * heads)
    log_sig = jnp.minimum(gt, 0.0) - jnp.log1p(jnp.exp(-jnp.abs(gt)))
    gt = jnp.where(is_forget, log_sig, gt)
    gc_ref[0] = gt
    gr_ref[0] = gt.T[0:gr_ref.shape[1], :]


def _mlstm_proj(x, norm_g, w_in, b_gate, *, heads, mv):
    b, l, d = x.shape
    ng = N_GATE_KINDS * heads
    mq = (w_in.shape[1] - 2 * mv - ng) // 2
    dqk = mq // heads
    tm = min(512, l)
    assert l % tm == 0 and ng <= V7X_LANES
    w_main = w_in[:, :2 * mq + 2 * mv].astype(BF16)
    w_gate = jnp.pad(w_in[:, 2 * mq + 2 * mv:], ((0, 0), (0, V7X_LANES - ng))).astype(BF16)
    b_pad = jnp.pad(b_gate, (0, V7X_LANES - ng)).reshape(1, V7X_LANES)
    kern = functools.partial(_proj_kernel, mq=mq, mv=mv, heads=heads, k_scale=float(dqk) ** -0.5)
    row = lambda bi, i: (bi, i, 0)
    vmem = w_main.size * 2 + w_gate.size * 2 + 2 * tm * d * 4 + 2 * tm * (2 * mq + 2 * mv) * 2 + 8 * tm * mv * 4
    return pl.pallas_call(
        kern,
        grid=(b, l // tm),
        in_specs=[
            pl.BlockSpec((1, tm, d), row),
            _resident((1, d)),
            _resident(w_main.shape),
            _resident(w_gate.shape),
            _resident((1, V7X_LANES)),
        ],
        out_specs=[
            pl.BlockSpec((1, tm, mq), row),
            pl.BlockSpec((1, tm, mq), row),
            pl.BlockSpec((1, tm, mv), row),
            pl.BlockSpec((1, tm, mv), row),
            pl.BlockSpec((1, tm, V7X_LANES), row),
            pl.BlockSpec((1, ng, tm), lambda bi, i: (bi, 0, i)),
        ],
        out_shape=[
            jax.ShapeDtypeStruct((b, l, mq), BF16),
            jax.ShapeDtypeStruct((b, l, mq), BF16),
            jax.ShapeDtypeStruct((b, l, mv), BF16),
            jax.ShapeDtypeStruct((b, l, mv), BF16),
            jax.ShapeDtypeStruct((b, l, V7X_LANES), F32),
            jax.ShapeDtypeStruct((b, ng, l), F32),
        ],
        compiler_params=pltpu.CompilerParams(
            dimension_semantics=("parallel", "parallel"), vmem_limit_bytes=_vmem_limit(vmem)),
        name="mlstm_proj",
    )(x, norm_g.reshape(1, d), w_main, w_gate, b_pad)


def _split3(x):
    hi = x.astype(BF16)
    r = x - hi.astype(F32)
    mid = r.astype(BF16)
    lo = (r - mid.astype(F32)).astype(BF16)
    return hi, mid, lo


def _scan_kernel(qf_ref, kf_ref, vf_ref, gcf_ref, grf_ref, qb_ref, kb_ref, vb_ref, gcb_ref, grb_ref,
                 hf_ref, hb_ref, c_ref, n_ref, m_ref, *, t, heads, dqk, dv):
    @pl.when(pl.program_id(1) == 0)
    def _():
        c_ref[...] = jnp.zeros_like(c_ref)
        n_ref[...] = jnp.zeros_like(n_ref)
        m_ref[...] = jnp.zeros_like(m_ref)

    row = lax.broadcasted_iota(jnp.int32, (t, t), 0)
    col = lax.broadcasted_iota(jnp.int32, (t, t), 1)
    causal = col <= row
    anti = col >= row
    causal_m = causal.astype(BF16)
    anti_m = anti.astype(BF16)

    streams = (
        (qf_ref, kf_ref, vf_ref, gcf_ref, grf_ref, hf_ref, causal, causal_m, anti_m, 0, t - 1),
        (qb_ref, kb_ref, vb_ref, gcb_ref, grb_ref, hb_ref, anti, anti_m, causal_m, 2 * heads, 0),
    )
    for di, (q_ref, k_ref, v_ref, gc_ref, gr_ref, h_ref, mask, tri, tri_t, base, end) in enumerate(streams):
        gc = gc_ref[0]
        gr = gr_ref[0]
        cs_c = sum(jnp.dot(tri, p, preferred_element_type=F32) for p in _split3(gc))
        cs_r = sum(jnp.dot(p, tri_t, preferred_element_type=F32) for p in _split3(gr))
        for h in range(heads):
            idx = di * heads + h
            ig_c = gc[:, base + h:base + h + 1]
            ig_r = gr[base + h:base + h + 1, :]
            b_c = cs_c[:, base + heads + h:base + heads + h + 1]
            b_r = cs_r[base + heads + h:base + heads + h + 1, :]
            total = b_c[end:end + 1, :]
            q = q_ref[0, :, h * dqk:(h + 1) * dqk]
            k = k_ref[0, :, h * dqk:(h + 1) * dqk]
            v = v_ref[0, :, h * dv:(h + 1) * dv]
            c_prev = c_ref[idx]
            n_prev = n_ref[idx]
            m_prev = m_ref[idx]

            inter = b_c + m_prev
            dlog = jnp.where(mask, b_c - b_r + ig_r, -jnp.inf)
            m_t = jnp.maximum(inter, jnp.max(dlog, axis=-1, keepdims=True))
            w = jnp.exp(dlog - m_t)
            s = lax.dot_general(q, k, (((1,), (1,)), ((), ())), preferred_element_type=F32) * w
            sc = jnp.exp(inter - m_t)
            num = (jnp.dot(s.astype(BF16), v, preferred_element_type=F32)
                   + sc * jnp.dot(q, c_prev.astype(BF16), preferred_element_type=F32))
            den = (jnp.sum(s, axis=-1, keepdims=True)
                   + sc * jnp.sum(q.astype(F32) * n_prev, axis=-1, keepdims=True))
            h_ref[0, :, h * dv:(h + 1) * dv] = num / jnp.maximum(jnp.abs(den), jnp.exp(-m_t))

            gk = total - b_c + ig_c
            m_new = jnp.maximum(total + m_prev, jnp.max(gk, axis=0, keepdims=True))
            wk = jnp.exp(gk - m_new)
            decay = jnp.exp(total + m_prev - m_new)
            vw = (v.astype(F32) * wk).astype(BF16)
            c_ref[idx] = decay * c_prev + lax.dot_general(
                k, vw, (((0,), (0,)), ((), ())), preferred_element_type=F32)
            n_ref[idx] = decay * n_prev + jnp.sum(k.astype(F32) * wk, axis=0, keepdims=True)
            m_ref[idx] = m_new


def _mlstm_scan(q, k, v, gc, gr, *, heads):
    b, l, mq = q.shape
    mv = v.shape[-1]
    dqk, dv = mq // heads, mv // heads
    t = min(SCAN_CHUNK, l)
    assert l % t == 0
    nc = l // t
    ng = gr.shape[1]
    kern = functools.partial(_scan_kernel, t=t, heads=heads, dqk=dqk, dv=dv)
    fwd = lambda bi, c: (bi, c, 0)
    bwd = lambda bi, c: (bi, nc - 1 - c, 0)
    fwd_r = lambda bi, c: (bi, 0, c)
    bwd_r = lambda bi, c: (bi, 0, nc - 1 - c)

    def stream(tok, rowm):
        return [pl.BlockSpec((1, t, mq), tok), pl.BlockSpec((1, t, mq), tok), pl.BlockSpec((1, t, mv), tok),
                pl.BlockSpec((1, t, V7X_LANES), tok), pl.BlockSpec((1, ng, t), rowm)]

    return pl.pallas_call(
        kern,
        grid=(b, nc),
        in_specs=stream(fwd, fwd_r) + stream(bwd, bwd_r),
        out_specs=[pl.BlockSpec((1, t, mv), fwd), pl.BlockSpec((1, t, mv), bwd)],
        out_shape=[jax.ShapeDtypeStruct((b, l, mv), F32), jax.ShapeDtypeStruct((b, l, mv), F32)],
        scratch_shapes=[pltpu.VMEM((2 * heads, dqk, dv), F32), pltpu.VMEM((2 * heads, 1, dqk), F32),
                        pltpu.VMEM((2 * heads, 1, 1), F32)],
        compiler_params=pltpu.CompilerParams(dimension_semantics=("parallel", "arbitrary")),
        name="mlstm_scan",
    )(q, k, v, gc, gr, q, k, v, gc, gr)


def _mout_kernel(hf_ref, hb_ref, o_ref, x_ref, hg_ref, w_ref, out_ref, y_ref, *, heads, dv):
    for h in range(heads):
        sl = slice(h * dv, (h + 1) * dv)
        hs = hf_ref[0, :, sl] + hb_ref[0, :, sl]
        hs = hs * lax.rsqrt(jnp.mean(hs * hs, axis=-1, keepdims=True) + EPS)
        hs = hs * hg_ref[:, sl]
        y_ref[:, sl] = (hs * jax.nn.sigmoid(o_ref[0, :, sl].astype(F32))).astype(BF16)
    out_ref[0] = x_ref[0] + jnp.dot(y_ref[...], w_ref[...], preferred_element_type=F32)


def _mlstm_out(hf, hb, o, x, head_g, w_out):
    b, l, d = x.shape
    heads, dv = head_g.shape
    mv = heads * dv
    tm = min(512, l)
    assert l % tm == 0
    row = lambda bi, i: (bi, i, 0)
    w = w_out.astype(BF16)
    vmem = w.size * 2 + 2 * tm * (2 * mv * 4 + mv * 2 + 2 * d * 4) + tm * mv * 2 + 4 * tm * d * 4
    return pl.pallas_call(
        functools.partial(_mout_kernel, heads=heads, dv=dv),
        grid=(b, l // tm),
        in_specs=[pl.BlockSpec((1, tm, mv), row), pl.BlockSpec((1, tm, mv), row), pl.BlockSpec((1, tm, mv), row),
                  pl.BlockSpec((1, tm, d), row), _resident((1, mv)), _resident(w.shape)],
        out_specs=pl.BlockSpec((1, tm, d), row),
        out_shape=jax.ShapeDtypeStruct((b, l, d), F32),
        scratch_shapes=[pltpu.VMEM((tm, mv), BF16)],
        compiler_params=pltpu.CompilerParams(
            dimension_semantics=("parallel", "parallel"), vmem_limit_bytes=_vmem_limit(vmem)),
        name="mlstm_out",
    )(hf, hb, o, x, head_g.reshape(1, mv), w)


def _dft_tables(l1, l2, gd):
    l = l1 * l2
    a1 = 2.0 * np.pi * np.outer(np.arange(l1), np.arange(l1)) / l1
    wa = np.concatenate([np.cos(a1), -np.sin(a1)], axis=0) / np.sqrt(l1)
    tw = 2.0 * np.pi * np.outer(np.arange(l2), np.arange(l1)) / l
    twr, twi = np.cos(tw)[..., None], -np.sin(tw)[..., None]
    a2 = 2.0 * np.pi * np.outer(np.arange(l2), np.arange(l2)) / l2
    c2, s2 = np.cos(a2), np.sin(a2)
    wc = np.block([[c2, s2], [-s2, c2]]) / np.sqrt(l2)
    ac = 2.0 * np.pi * np.outer(np.arange(gd), np.arange(gd)) / gd
    cs = np.stack([np.cos(ac), np.sin(ac)]) / np.sqrt(gd)
    return (jnp.asarray(wa, BF16), jnp.asarray(twr, F32), jnp.asarray(twi, F32),
            jnp.asarray(wc, BF16), jnp.asarray(cs, BF16))


def _fa_kernel(x_ref, g_ref, wa_ref, twr_ref, twi_ref, out_ref, h_ref, *, l1, n2b, d):
    g = g_ref[...]
    for j in range(n2b):
        h_ref[:, j * d:(j + 1) * d] = _rmsnorm(x_ref[0, :, j * d:(j + 1) * d], g).astype(BF16)
    a = jnp.dot(wa_ref[...], h_ref[...], preferred_element_type=F32)
    for j in range(n2b):
        ar = a[:l1, j * d:(j + 1) * d]
        ai = a[l1:, j * d:(j + 1) * d]
        tr = twr_ref[j]
        ti = twi_ref[j]
        out_ref[0, 0, j] = (ar * tr - ai * ti).astype(BF16)
        out_ref[0, 1, j] = (ar * ti + ai * tr).astype(BF16)


def _fc_kernel(ba_ref, x_ref, wc_ref, cs_ref, wout_ref, bout_ref, out_ref, z_ref, f_ref, *, l2, k1b, d, gd):
    for j in range(k1b):
        z = jnp.dot(wc_ref[...], ba_ref[0, :, j * d:(j + 1) * d], preferred_element_type=F32)
        z_ref[0, j * l2:(j + 1) * l2, :] = z[:l2].astype(BF16)
        z_ref[1, j * l2:(j + 1) * l2, :] = z[l2:].astype(BF16)
    for gi in range(d // gd):
        sl = slice(gi * gd, (gi + 1) * gd)
        f = (jnp.dot(z_ref[0, :, sl], cs_ref[0], preferred_element_type=F32)
             + jnp.dot(z_ref[1, :, sl], cs_ref[1], preferred_element_type=F32))
        f_ref[:, sl] = f.astype(BF16)
    y = jnp.dot(f_ref[...], wout_ref[...], preferred_element_type=F32) + bout_ref[...]
    for j in range(k1b):
        out_ref[0, :, j * d:(j + 1) * d] = x_ref[0, :, j * d:(j + 1) * d] + y[j * l2:(j + 1) * l2]


def _fourier_mixer(x, norm_g, w_out, b_out):
    b, l, d = x.shape
    l2 = min(DFT_INNER, l)
    l1 = l // l2
    assert l1 * l2 == l and d % F_GROUPS == 0
    gd = d // F_GROUPS
    wa, twr, twi, wc, cs = _dft_tables(l1, l2, gd)
    n2b = min(8, l2)
    k1b = min(8, l1)
    assert l2 % n2b == 0 and l1 % k1b == 0

    ba = pl.pallas_call(
        functools.partial(_fa_kernel, l1=l1, n2b=n2b, d=d),
        grid=(b, l2 // n2b),
        in_specs=[pl.BlockSpec((1, l1, n2b * d), lambda bi, j: (bi, 0, j)),
                  _resident((1, d)), _resident(wa.shape),
                  pl.BlockSpec((n2b, l1, 1), lambda bi, j: (j, 0, 0)),
                  pl.BlockSpec((n2b, l1, 1), lambda bi, j: (j, 0, 0))],
        out_specs=pl.BlockSpec((1, 2, n2b, l1, d), lambda bi, j: (bi, 0, j, 0, 0)),
        out_shape=jax.ShapeDtypeStruct((b, 2, l2, l1, d), BF16),
        scratch_shapes=[pltpu.VMEM((l1, n2b * d), BF16)],
        compiler_params=pltpu.CompilerParams(
            dimension_semantics=("parallel", "parallel"),
            vmem_limit_bytes=_vmem_limit(l1 * n2b * d * (2 * 4 + 2 + 2 * 4 * 3 + 2 * 2 * 2))),
        name="fourier_seq_a",
    )(x.reshape(b, l1, l2 * d), norm_g.reshape(1, d), wa, twr, twi)

    rows = k1b * l2
    out = pl.pallas_call(
        functools.partial(_fc_kernel, l2=l2, k1b=k1b, d=d, gd=gd),
        grid=(b, l1 // k1b),
        in_specs=[pl.BlockSpec((1, 2 * l2, k1b * d), lambda bi, j: (bi, 0, j)),
                  pl.BlockSpec((1, l2, k1b * d), lambda bi, j: (bi, 0, j)),
                  _resident(wc.shape), _resident(cs.shape), _resident((d, d)), _resident((1, d))],
        out_specs=pl.BlockSpec((1, l2, k1b * d), lambda bi, j: (bi, 0, j)),
        out_shape=jax.ShapeDtypeStruct((b, l2, l1 * d), F32),
        scratch_shapes=[pltpu.VMEM((2, rows, d), BF16), pltpu.VMEM((rows, d), BF16)],
        compiler_params=pltpu.CompilerParams(
            dimension_semantics=("parallel", "parallel"),
            vmem_limit_bytes=_vmem_limit(rows * d * (2 * 2 * 2 + 4 * 4 + 3 * 2 + 4 * 4) + 4 * d * d)),
        name="fourier_seq_c",
    )(ba.reshape(b, 2 * l2, l1 * d), x.reshape(b, l2, l1 * d), wc, cs, w_out.astype(BF16), b_out.reshape(1, d))
    return out.reshape(b, l, d)


def _mlstm_mixer(x, norm_g, w_in, b_gate, head_g, w_out):
    heads, dv = head_g.shape
    q, k, v, o, gc, gr = _mlstm_proj(x, norm_g, w_in, b_gate, heads=heads, mv=heads * dv)
    hf, hb = _mlstm_scan(q, k, v, gc, gr, heads=heads)
    return _mlstm_out(hf, hb, o, x, head_g, w_out)


def _trunk(x, mix_norm_g, m_w_in, m_b_gate, m_head_g, m_w_out, f_w_out, f_b_out,
           ffn_norm_g, ffn_w_up, ffn_conv_w, ffn_conv_b, ffn_w_down, final_norm_g):
    depth = mix_norm_g.shape[0]
    n_mixers = 2
    for i in range(depth):
        j = i // n_mixers
        if i % n_mixers == 0:
            x = _mlstm_mixer(x, mix_norm_g[i], m_w_in[j], m_b_gate[j], m_head_g[j], m_w_out[j])
        else:
            x = _fourier_mixer(x, mix_norm_g[i], f_w_out[j], f_b_out[j])
        x = _conv_ffn(x, ffn_norm_g[i], ffn_w_up[i], ffn_conv_w[i], ffn_conv_b[i], ffn_w_down[i],
                      final_norm_g, final_norm=(i == depth - 1))
    return x


def kernel(x_prompt, x_sample, mix_norm_g, m_w_in, m_b_gate, m_head_g, m_w_out, f_w_out, f_b_out,
           ffn_norm_g, ffn_w_up, ffn_conv_w, ffn_conv_b, ffn_w_down, final_norm_g):
    params = (mix_norm_g, m_w_in, m_b_gate, m_head_g, m_w_out, f_w_out, f_b_out,
              ffn_norm_g, ffn_w_up, ffn_conv_w, ffn_conv_b, ffn_w_down, final_norm_g)
    return (_trunk(x_prompt, *params), _trunk(x_sample, *params))
```

```python
import functools

import numpy as np
import jax
import jax.numpy as jnp
from jax import lax
from jax.experimental import pallas as pl
from jax.experimental.pallas import tpu as pltpu

F32 = jnp.float32
BF16 = jnp.bfloat16

EPS = 1e-6
F_GROUPS = 4
N_GATE_KINDS = 4

V7X_VMEM_BYTES = 64 * 1024 * 1024
V7X_LANES = 128
V7X_SUBLANES = 8
V7X_BF16_ROWS = 16
V7X_MXU_DIM = 256

HALO = V7X_SUBLANES
SCAN_CHUNK = 128
DFT_INNER = 128


def _vmem_limit(nbytes):
    return int(min(max(nbytes, 32 * 1024 * 1024), V7X_VMEM_BYTES - 8 * 1024 * 1024))


def _rmsnorm(x, g):
    y = x * lax.rsqrt(jnp.mean(x * x, axis=-1, keepdims=True) + EPS)
    return y * g


def _resident(shape):
    nd = len(shape)
    return pl.BlockSpec(shape, lambda *_: (0,) * nd, pipeline_mode=pl.Buffered(1))


def _ffn_kernel(xm_ref, xp_ref, xn_ref, g_ref, wup_ref, cw_ref, cb_ref, wdn_ref, fg_ref,
                o_ref, h_ref, act_ref, *, tm, n_chunks, fc, conv_w, final_norm):
    i = pl.program_id(1)
    last = pl.num_programs(1) - 1
    g = g_ref[...]
    x = xm_ref[0]
    hp = jnp.where(i > 0, _rmsnorm(xp_ref[0], g), 0.0)
    hn = jnp.where(i < last, _rmsnorm(xn_ref[0], g), 0.0)
    h_ref[0:HALO, :] = hp.astype(BF16)
    h_ref[HALO:HALO + tm, :] = _rmsnorm(x, g).astype(BF16)
    h_ref[HALO + tm:, :] = hn.astype(BF16)
    hb = h_ref[...]
    pad = conv_w // 2
    for c in range(n_chunks):
        u = jnp.dot(hb, wup_ref[c], preferred_element_type=F32)
        w = cw_ref[c]
        y = cb_ref[c]
        for j in range(conv_w):
            off = HALO - pad + j
            y = y + u[off:off + tm] * w[j:j + 1]
        a = y[:, :fc]
        val = y[:, fc:]
        act_ref[:, c * fc:(c + 1) * fc] = (a * jax.nn.sigmoid(a) * val).astype(BF16)
    r = x + jnp.dot(act_ref[...], wdn_ref[...], preferred_element_type=F32)
    if final_norm:
        r = _rmsnorm(r, fg_ref[...])
    o_ref[0] = r


def _conv_ffn(x, norm_g, w_up, conv_w, conv_b, w_down, final_g, *, final_norm):
    b, l, d = x.shape
    d_ff = w_down.shape[0]
    cw = conv_w.shape[0]
    fc = V7X_MXU_DIM
    n_chunks = d_ff // fc
    assert n_chunks * fc == d_ff and cw // 2 <= HALO
    tm = min(512, l)
    assert l % tm == 0 and tm % HALO == 0
    nt = l // tm
    rb = tm // HALO
    wup = w_up.reshape(d, 2, n_chunks, fc).transpose(2, 0, 1, 3).reshape(n_chunks, d, 2 * fc).astype(BF16)
    cwr = conv_w.reshape(cw, 2, n_chunks, fc).transpose(2, 0, 1, 3).reshape(n_chunks, cw, 2 * fc)
    cbr = conv_b.reshape(2, n_chunks, fc).transpose(1, 0, 2).reshape(n_chunks, 1, 2 * fc)
    wdn = w_down.astype(BF16)
    kern = functools.partial(_ffn_kernel, tm=tm, n_chunks=n_chunks, fc=fc, conv_w=cw, final_norm=final_norm)
    vmem = (wup.size * 2 + wdn.size * 2 + 4 * tm * d * 4 + (tm + 2 * HALO) * d * 2 + tm * d_ff * 2
            + 6 * (tm + 2 * HALO) * 2 * fc * 4 + 4 * tm * d * 4)
    return pl.pallas_call(
        kern,
        grid=(b, nt),
        in_specs=[
            pl.BlockSpec((1, tm, d), lambda bi, i: (bi, i, 0)),
            pl.BlockSpec((1, HALO, d), lambda bi, i: (bi, jnp.maximum(i * rb - 1, 0), 0)),
            pl.BlockSpec((1, HALO, d), lambda bi, i: (bi, jnp.minimum((i + 1) * rb, l // HALO - 1), 0)),
            _resident((1, d)),
            _resident(wup.shape),
            _resident(cwr.shape),
            _resident(cbr.shape),
            _resident(wdn.shape),
            _resident((1, d)),
        ],
        out_specs=pl.BlockSpec((1, tm, d), lambda bi, i: (bi, i, 0)),
        out_shape=jax.ShapeDtypeStruct((b, l, d), F32),
        scratch_shapes=[pltpu.VMEM((tm + 2 * HALO, d), BF16), pltpu.VMEM((tm, d_ff), BF16)],
        compiler_params=pltpu.CompilerParams(
            dimension_semantics=("parallel", "arbitrary"), vmem_limit_bytes=_vmem_limit(vmem)),
        name="conv_ffn",
    )(x, x, x, norm_g.reshape(1, d), wup, cwr, cbr, wdn, final_g.reshape(1, d))


def _split3(x):
    hi = x.astype(BF16)
    r = x - hi.astype(F32)
    mid = r.astype(BF16)
    lo = (r - mid.astype(F32)).astype(BF16)
    return hi, mid, lo


def _proj_kernel(x_ref, g_ref, w_ref, wg_ref, bg_ref, pre_ref, suf_ref, q_ref, k_ref, v_ref, o_ref, gc_ref, gr_ref,
                 *, mq, mv, heads, k_scale):
    h = _rmsnorm(x_ref[0], g_ref[...]).astype(BF16)
    q_ref[0] = jnp.dot(h, w_ref[:, 0:mq], preferred_element_type=F32).astype(BF16)
    k_ref[0] = (jnp.dot(h, w_ref[:, mq:2 * mq], preferred_element_type=F32) * k_scale).astype(BF16)
    v_ref[0] = jnp.dot(h, w_ref[:, 2 * mq:2 * mq + mv], preferred_element_type=F32).astype(BF16)
    o_ref[0] = jnp.dot(h, w_ref[:, 2 * mq + mv:], preferred_element_type=F32).astype(BF16)
    gt = jnp.dot(h, wg_ref[...], preferred_element_type=F32) + bg_ref[...]
    lane = lax.broadcasted_iota(jnp.int32, gt.shape, 1)
    kind = lane // heads
    log_sig = jnp.minimum(gt, 0.0) - jnp.log1p(jnp.exp(-jnp.abs(gt)))
    parts = _split3(log_sig)
    pre = sum(jnp.dot(pre_ref[...], p, preferred_element_type=F32) for p in parts)
    suf = sum(jnp.dot(suf_ref[...], p, preferred_element_type=F32) for p in parts)
    gt = jnp.where(kind == 1, pre, jnp.where(kind == 3, suf, gt))
    gc_ref[0] = gt
    gr_ref[0] = gt.T[0:gr_ref.shape[1], :]


def _mlstm_proj(x, norm_g, w_in, b_gate, *, heads, mv):
    b, l, d = x.shape
    ng = N_GATE_KINDS * heads
    mq = (w_in.shape[1] - 2 * mv - ng) // 2
    dqk = mq // heads
    tm = min(512, l)
    assert l % tm == 0 and ng <= V7X_LANES
    w_main = w_in[:, :2 * mq + 2 * mv].astype(BF16)
    w_gate = jnp.pad(w_in[:, 2 * mq + 2 * mv:], ((0, 0), (0, V7X_LANES - ng))).astype(BF16)
    b_pad = jnp.pad(b_gate, (0, V7X_LANES - ng)).reshape(1, V7X_LANES)
    t = min(SCAN_CHUNK, l)
    assert tm % t == 0
    pos = np.arange(tm)
    same_chunk = (pos[:, None] // t) == (pos[None, :] // t)
    pre_m = jnp.asarray(same_chunk & (pos[None, :] <= pos[:, None]), BF16)
    suf_m = jnp.asarray(same_chunk & (pos[None, :] >= pos[:, None]), BF16)
    kern = functools.partial(_proj_kernel, mq=mq, mv=mv, heads=heads, k_scale=float(dqk) ** -0.5)
    row = lambda bi, i: (bi, i, 0)
    vmem = (w_main.size * 2 + w_gate.size * 2 + 2 * tm * d * 4 + 2 * tm * (2 * mq + 2 * mv) * 2 + 8 * tm * mv * 4
            + 2 * tm * tm * 2)
    return pl.pallas_call(
        kern,
        grid=(b, l // tm),
        in_specs=[
            pl.BlockSpec((1, tm, d), row),
            _resident((1, d)),
            _resident(w_main.shape),
            _resident(w_gate.shape),
            _resident((1, V7X_LANES)),
            _resident((tm, tm)),
            _resident((tm, tm)),
        ],
        out_specs=[
            pl.BlockSpec((1, tm, mq), row),
            pl.BlockSpec((1, tm, mq), row),
            pl.BlockSpec((1, tm, mv), row),
            pl.BlockSpec((1, tm, mv), row),
            pl.BlockSpec((1, tm, V7X_LANES), row),
            pl.BlockSpec((1, ng, tm), lambda bi, i: (bi, 0, i)),
        ],
        out_shape=[
            jax.ShapeDtypeStruct((b, l, mq), BF16),
            jax.ShapeDtypeStruct((b, l, mq), BF16),
            jax.ShapeDtypeStruct((b, l, mv), BF16),
            jax.ShapeDtypeStruct((b, l, mv), BF16),
            jax.ShapeDtypeStruct((b, l, V7X_LANES), F32),
            jax.ShapeDtypeStruct((b, ng, l), F32),
        ],
        compiler_params=pltpu.CompilerParams(
            dimension_semantics=("parallel", "parallel"), vmem_limit_bytes=_vmem_limit(vmem)),
        name="mlstm_proj",
    )(x, norm_g.reshape(1, d), w_main, w_gate, b_pad, pre_m, suf_m)


def _scan_kernel(qf_ref, kf_ref, vf_ref, gcf_ref, grf_ref, qb_ref, kb_ref, vb_ref, gcb_ref, grb_ref,
                 hf_ref, hb_ref, c_ref, rhs_ref, n_ref, m_ref, *, t, heads, dqk, dv):
    step = pl.program_id(1)
    par = step % 2

    @pl.when(step == 0)
    def _():
        c_ref[...] = jnp.zeros_like(c_ref)
        rhs_ref[...] = jnp.zeros_like(rhs_ref)
        n_ref[...] = jnp.zeros_like(n_ref)
        m_ref[...] = jnp.zeros_like(m_ref)

    row = lax.broadcasted_iota(jnp.int32, (t, t), 0)
    col = lax.broadcasted_iota(jnp.int32, (t, t), 1)
    first_head = lax.broadcasted_iota(jnp.int32, (t, 2 * dqk), 1) < dqk
    streams = (
        (qf_ref, kf_ref, vf_ref, gcf_ref, grf_ref, hf_ref, col <= row, 0, t - 1),
        (qb_ref, kb_ref, vb_ref, gcb_ref, grb_ref, hb_ref, col >= row, 2 * heads, 0),
    )
    dn_nt = (((1,), (1,)), ((), ()))
    dn_tn = (((0,), (0,)), ((), ()))

    st = {}
    for di, (q_ref, k_ref, v_ref, gc_ref, gr_ref, h_ref, mask, base, end) in enumerate(streams):
        gc = gc_ref[0]
        gr = gr_ref[0]
        for h in range(heads):
            idx = di * heads + h
            ig_c = gc[:, base + h:base + h + 1]
            b_c = gc[:, base + heads + h:base + heads + h + 1]
            src_r = gr[base + h:base + h + 1, :] - gr[base + heads + h:base + heads + h + 1, :]
            total = b_c[end:end + 1, :]
            m_prev = m_ref[idx]
            st[idx] = dict(b_c=b_c, src_r=src_r, inter=b_c + m_prev, n_prev=n_ref[idx],
                           gk=total - b_c + ig_c, carry=total + m_prev)
        for p in range(heads // 2):
            sl = slice(2 * p * dqk, (2 * p + 2) * dqk)
            k2 = k_ref[0, :, sl]
            zero = jnp.zeros_like(k2)
            kk = jnp.concatenate([jnp.where(first_head, k2, zero), jnp.where(first_head, zero, k2)], axis=0)
            s2 = lax.dot_general(q_ref[0, :, sl], kk, dn_nt, preferred_element_type=F32)
            st[di * heads + 2 * p]["qk"] = s2[:, :t]
            st[di * heads + 2 * p + 1]["qk"] = s2[:, t:]

    every = [(di, h, di * heads + h) + streams[di][:1] + streams[di][5:7]
             for di in range(len(streams)) for h in range(heads)]
    for di, h, idx, q_ref, h_ref, mask in every:
        e = st[idx]
        m_new = jnp.maximum(e["carry"], jnp.max(e["gk"], axis=0, keepdims=True))
        m_ref[idx] = m_new
        e["wk"] = jnp.exp(e["gk"] - m_new)
        e["decay"] = jnp.exp(e["carry"] - m_new)
    for di, h, idx, q_ref, h_ref, mask in every:
        e = st[idx]
        e["v"] = streams[di][2][0, :, h * dv:(h + 1) * dv]
        e["kw"] = streams[di][1][0, :, h * dqk:(h + 1) * dqk].astype(F32) * e["wk"]
    for di, h, idx, q_ref, h_ref, mask in every:
        e = st[idx]
        e["upd"] = lax.dot_general(e["kw"].astype(BF16), e["v"], dn_tn, preferred_element_type=F32)
        n_ref[idx] = e["decay"] * e["n_prev"] + jnp.sum(e["kw"], axis=0, keepdims=True)
        rhs_ref[par, idx, :t, :] = e["v"]
    for di, h, idx, q_ref, h_ref, mask in every:
        e = st[idx]
        c_new = e["decay"] * c_ref[idx] + e["upd"]
        c_ref[idx] = c_new
        rhs_ref[1 - par, idx, t:, :] = c_new.astype(BF16)

    for di, h, idx, q_ref, h_ref, mask in every:
        e = st[idx]
        e["q32"] = q_ref[0, :, h * dqk:(h + 1) * dqk].astype(F32)
        e["dlog"] = jnp.where(mask, e["b_c"] + e["src_r"], -jnp.inf)
    for di, h, idx, q_ref, h_ref, mask in every:
        e = st[idx]
        e["m_t"] = jnp.maximum(e["inter"], jnp.max(e["dlog"], axis=-1, keepdims=True))
        e["qn"] = jnp.sum(e["q32"] * e["n_prev"], axis=-1, keepdims=True)
    for di, h, idx, q_ref, h_ref, mask in every:
        e = st[idx]
        e["s"] = e["qk"] * jnp.exp(e["dlog"] - e["m_t"])
        e["sc"] = jnp.exp(e["inter"] - e["m_t"])
    for di, h, idx, q_ref, h_ref, mask in every:
        e = st[idx]
        den = jnp.sum(e["s"], axis=-1, keepdims=True) + e["sc"] * e["qn"]
        e["r"] = 1.0 / jnp.maximum(jnp.abs(den), jnp.exp(-e["m_t"]))
    for di, h, idx, q_ref, h_ref, mask in every:
        e = st[idx]
        e["lhs"] = jnp.concatenate(
            [(e["s"] * e["r"]).astype(BF16), (e["q32"] * (e["sc"] * e["r"])).astype(BF16)], axis=1)
    for di, h, idx, q_ref, h_ref, mask in every:
        h_ref[0, :, h * dv:(h + 1) * dv] = jnp.dot(
            st[idx]["lhs"], rhs_ref[par, idx], preferred_element_type=F32).astype(h_ref.dtype)


def _mlstm_scan(q, k, v, gc, gr, *, heads):
    b, l, mq = q.shape
    mv = v.shape[-1]
    dqk, dv = mq // heads, mv // heads
    t = min(SCAN_CHUNK, l)
    assert l % t == 0 and heads % 2 == 0
    nc = l // t
    ng = gr.shape[1]
    kern = functools.partial(_scan_kernel, t=t, heads=heads, dqk=dqk, dv=dv)
    fwd = lambda bi, c: (bi, c, 0)
    bwd = lambda bi, c: (bi, nc - 1 - c, 0)
    fwd_r = lambda bi, c: (bi, 0, c)
    bwd_r = lambda bi, c: (bi, 0, nc - 1 - c)

    def stream(tok, rowm):
        return [pl.BlockSpec((1, t, mq), tok), pl.BlockSpec((1, t, mq), tok), pl.BlockSpec((1, t, mv), tok),
                pl.BlockSpec((1, t, V7X_LANES), tok), pl.BlockSpec((1, ng, t), rowm)]

    return pl.pallas_call(
        kern,
        grid=(b, nc),
        in_specs=stream(fwd, fwd_r) + stream(bwd, bwd_r),
        out_specs=[pl.BlockSpec((1, t, mv), fwd), pl.BlockSpec((1, t, mv), bwd)],
        out_shape=[jax.ShapeDtypeStruct((b, l, mv), BF16), jax.ShapeDtypeStruct((b, l, mv), BF16)],
        scratch_shapes=[pltpu.VMEM((2 * heads, dqk, dv), F32), pltpu.VMEM((2, 2 * heads, t + dqk, dv), BF16),
                        pltpu.VMEM((2 * heads, 1, dqk), F32), pltpu.VMEM((2 * heads, 1, 1), F32)],
        compiler_params=pltpu.CompilerParams(dimension_semantics=("parallel", "arbitrary")),
        name="mlstm_scan",
    )(q, k, v, gc, gr, q, k, v, gc, gr)


def _mout_kernel(hf_ref, hb_ref, o_ref, x_ref, hg_ref, w_ref, out_ref, y_ref, *, heads, dv):
    for h in range(heads):
        sl = slice(h * dv, (h + 1) * dv)
        hs = hf_ref[0, :, sl].astype(F32) + hb_ref[0, :, sl].astype(F32)
        hs = hs * lax.rsqrt(jnp.mean(hs * hs, axis=-1, keepdims=True) + EPS)
        hs = hs * hg_ref[:, sl]
        y_ref[:, sl] = (hs * jax.nn.sigmoid(o_ref[0, :, sl].astype(F32))).astype(BF16)
    out_ref[0] = x_ref[0] + jnp.dot(y_ref[...], w_ref[...], preferred_element_type=F32)


def _mlstm_out(hf, hb, o, x, head_g, w_out):
    b, l, d = x.shape
    heads, dv = head_g.shape
    mv = heads * dv
    tm = min(512, l)
    assert l % tm == 0
    row = lambda bi, i: (bi, i, 0)
    w = w_out.astype(BF16)
    vmem = w.size * 2 + 2 * tm * (3 * mv * 2 + 2 * d * 4) + tm * mv * 2 + 6 * tm * d * 4
    return pl.pallas_call(
        functools.partial(_mout_kernel, heads=heads, dv=dv),
        grid=(b, l // tm),
        in_specs=[pl.BlockSpec((1, tm, mv), row), pl.BlockSpec((1, tm, mv), row), pl.BlockSpec((1, tm, mv), row),
                  pl.BlockSpec((1, tm, d), row), _resident((1, mv)), _resident(w.shape)],
        out_specs=pl.BlockSpec((1, tm, d), row),
        out_shape=jax.ShapeDtypeStruct((b, l, d), F32),
        scratch_shapes=[pltpu.VMEM((tm, mv), BF16)],
        compiler_params=pltpu.CompilerParams(
            dimension_semantics=("parallel", "parallel"), vmem_limit_bytes=_vmem_limit(vmem)),
        name="mlstm_out",
    )(hf, hb, o, x, head_g.reshape(1, mv), w)


def _dft_tables(l1, l2, gd, n2h, n2b):
    l = l1 * l2
    a1 = 2.0 * np.pi * np.outer(np.arange(l1), np.arange(l1)) / l1
    wa = np.stack([np.cos(a1), -np.sin(a1)]) / np.sqrt(l1)
    wa = np.einsum("rkn,ij->krinj", wa, np.eye(n2h)).reshape(l1 * 2 * n2h, l1 * n2h)
    tw = 2.0 * np.pi * np.einsum("k,jhi->jhki", np.arange(l1),
                                 np.arange(l2).reshape(l2 // n2b, n2b // n2h, n2h)) / l
    twr, twi = np.cos(tw)[..., None], -np.sin(tw)[..., None]
    a2 = 2.0 * np.pi * np.outer(np.arange(l2), np.arange(l2)) / l2
    c2, s2 = np.cos(a2), np.sin(a2)
    wc = np.block([[c2, s2], [-s2, c2]]) / np.sqrt(l2)
    ac = 2.0 * np.pi * np.outer(np.arange(gd), np.arange(gd)) / gd
    cs = np.stack([np.cos(ac), np.sin(ac)]) / np.sqrt(gd)
    return (jnp.asarray(wa, BF16), jnp.asarray(twr, F32), jnp.asarray(twi, F32),
            jnp.asarray(wc, BF16), jnp.asarray(cs, BF16))


def _fa_kernel(x_ref, g_ref, wa_ref, twr_ref, twi_ref, out_ref, *, l1, n2h, n2b, d):
    g = g_ref[...]
    wa = wa_ref[...]
    halves = []
    for hh in range(n2b // n2h):
        x = x_ref[0, :, hh * n2h:(hh + 1) * n2h, :].reshape(l1 * n2h, d)
        a = jnp.dot(wa, _rmsnorm(x, g).astype(BF16), preferred_element_type=F32)
        a = a.reshape(l1, 2, n2h, d)
        ar = a[:, 0]
        ai = a[:, 1]
        tr = twr_ref[0, hh]
        ti = twi_ref[0, hh]
        halves.append(jnp.stack([ar * tr - ai * ti, ar * ti + ai * tr], axis=1))
    out_ref[0] = jnp.concatenate(halves, axis=2).astype(BF16)


def _fc_kernel(ba_ref, x_ref, wc_ref, cs_ref, wout_ref, bout_ref, out_ref, z_ref, f_ref, *, l2, k1b, d, gd):
    wc = wc_ref[...]
    for j in range(k1b):
        z = jnp.dot(wc, ba_ref[0, j], preferred_element_type=F32)
        z_ref[0, j * l2:(j + 1) * l2, :] = z[:l2].astype(BF16)
        z_ref[1, j * l2:(j + 1) * l2, :] = z[l2:].astype(BF16)
    for gi in range(d // gd):
        sl = slice(gi * gd, (gi + 1) * gd)
        f = (jnp.dot(z_ref[0, :, sl], cs_ref[0], preferred_element_type=F32)
             + jnp.dot(z_ref[1, :, sl], cs_ref[1], preferred_element_type=F32))
        f_ref[:, sl] = f.astype(BF16)
    y = jnp.dot(f_ref[...], wout_ref[...], preferred_element_type=F32) + bout_ref[...]
    for j in range(k1b):
        out_ref[0, :, j, :] = x_ref[0, :, j, :] + y[j * l2:(j + 1) * l2]


def _fourier_mixer(x, norm_g, w_out, b_out):
    b, l, d = x.shape
    l2 = min(DFT_INNER, l)
    l1 = l // l2
    assert l1 * l2 == l and d % F_GROUPS == 0
    gd = d // F_GROUPS
    n2h = V7X_SUBLANES
    n2b = min(V7X_BF16_ROWS, l2)
    k1b = min(V7X_SUBLANES, l1)
    assert l2 % n2b == 0 and n2b % n2h == 0 and l1 % k1b == 0
    wa, twr, twi, wc, cs = _dft_tables(l1, l2, gd, n2h, n2b)

    tw_spec = pl.BlockSpec((1, n2b // n2h, l1, n2h, 1), lambda bi, j: (j, 0, 0, 0, 0))
    ba = pl.pallas_call(
        functools.partial(_fa_kernel, l1=l1, n2h=n2h, n2b=n2b, d=d),
        grid=(b, l2 // n2b),
        in_specs=[pl.BlockSpec((1, l1, n2b, d), lambda bi, j: (bi, 0, j, 0)),
                  _resident((1, d)), _resident(wa.shape), tw_spec, tw_spec],
        out_specs=pl.BlockSpec((1, l1, 2, n2b, d), lambda bi, j: (bi, 0, 0, j, 0)),
        out_shape=jax.ShapeDtypeStruct((b, l1, 2, l2, d), BF16),
        compiler_params=pltpu.CompilerParams(
            dimension_semantics=("parallel", "parallel"),
            vmem_limit_bytes=_vmem_limit(l1 * n2b * d * (2 * 4 + 2 * 2 * 2 + 6 * 4) + wa.size * 2)),
        name="fourier_seq_a",
    )(x.reshape(b, l1, l2, d), norm_g.reshape(1, d), wa, twr, twi)

    rows = k1b * l2
    out = pl.pallas_call(
        functools.partial(_fc_kernel, l2=l2, k1b=k1b, d=d, gd=gd),
        grid=(b, l1 // k1b),
        in_specs=[pl.BlockSpec((1, k1b, 2 * l2, d), lambda bi, j: (bi, j, 0, 0)),
                  pl.BlockSpec((1, l2, k1b, d), lambda bi, j: (bi, 0, j, 0)),
                  _resident(wc.shape), _resident(cs.shape), _resident((d, d)), _resident((1, d))],
        out_specs=pl.BlockSpec((1, l2, k1b, d), lambda bi, j: (bi, 0, j, 0)),
        out_shape=jax.ShapeDtypeStruct((b, l2, l1, d), F32),
        scratch_shapes=[pltpu.VMEM((2, rows, d), BF16), pltpu.VMEM((rows, d), BF16)],
        compiler_params=pltpu.CompilerParams(
            dimension_semantics=("parallel", "parallel"),
            vmem_limit_bytes=_vmem_limit(rows * d * (2 * 2 * 2 + 4 * 4 + 3 * 2 + 4 * 4) + 4 * d * d)),
        name="fourier_seq_c",
    )(ba.reshape(b, l1, 2 * l2, d), x.reshape(b, l2, l1, d), wc, cs, w_out.astype(BF16), b_out.reshape(1, d))
    return out.reshape(b, l, d)


def _mlstm_mixer(x, norm_g, w_in, b_gate, head_g, w_out):
    heads, dv = head_g.shape
    q, k, v, o, gc, gr = _mlstm_proj(x, norm_g, w_in, b_gate, heads=heads, mv=heads * dv)
    hf, hb = _mlstm_scan(q, k, v, gc, gr, heads=heads)
    return _mlstm_out(hf, hb, o, x, head_g, w_out)


def _trunk(x, mix_norm_g, m_w_in, m_b_gate, m_head_g, m_w_out, f_w_out, f_b_out,
           ffn_norm_g, ffn_w_up, ffn_conv_w, ffn_conv_b, ffn_w_down, final_norm_g):
    depth = mix_norm_g.shape[0]
    n_mixers = 2
    for i in range(depth):
        j = i // n_mixers
        if i % n_mixers == 0:
            x = _mlstm_mixer(x, mix_norm_g[i], m_w_in[j], m_b_gate[j], m_head_g[j], m_w_out[j])
        else:
            x = _fourier_mixer(x, mix_norm_g[i], f_w_out[j], f_b_out[j])
        x = _conv_ffn(x, ffn_norm_g[i], ffn_w_up[i], ffn_conv_w[i], ffn_conv_b[i], ffn_w_down[i],
                      final_norm_g, final_norm=(i == depth - 1))
    return x


def kernel(x_prompt, x_sample, mix_norm_g, m_w_in, m_b_gate, m_head_g, m_w_out, f_w_out, f_b_out,
           ffn_norm_g, ffn_w_up, ffn_conv_w, ffn_conv_b, ffn_w_down, final_norm_g):
    params = (mix_norm_g, m_w_in, m_b_gate, m_head_g, m_w_out, f_w_out, f_b_out,
              ffn_norm_g, ffn_w_up, ffn_conv_w, ffn_conv_b, ffn_w_down, final_norm_g)
    return (_trunk(x_prompt, *params), _trunk(x_sample, *params))
```

```python
import functools

import numpy as np
import jax
import jax.numpy as jnp
from jax import lax
from jax.experimental import pallas as pl
from jax.experimental.pallas import tpu as pltpu

F32 = jnp.float32
BF16 = jnp.bfloat16

EPS = 1e-6
F_GROUPS = 4
N_GATE_KINDS = 4

V7X_VMEM_BYTES = 64 * 1024 * 1024
V7X_LANES = 128
V7X_SUBLANES = 8
V7X_BF16_ROWS = 16
V7X_MXU_DIM = 256

HALO = V7X_SUBLANES
SCAN_CHUNK = 128
N_ROWS = V7X_BF16_ROWS
DFT_INNER = 128


def _vmem_limit(nbytes):
    return int(min(max(nbytes, 32 * 1024 * 1024), V7X_VMEM_BYTES - 8 * 1024 * 1024))


def _rmsnorm(x, g):
    y = x * lax.rsqrt(jnp.mean(x * x, axis=-1, keepdims=True) + EPS)
    return y * g


def _resident(shape):
    nd = len(shape)
    return pl.BlockSpec(shape, lambda *_: (0,) * nd, pipeline_mode=pl.Buffered(1))


def _ffn_kernel(xm_ref, xp_ref, xn_ref, g_ref, wup_ref, cw_ref, cb_ref, wdn_ref, fg_ref,
                o_ref, h_ref, act_ref, *, tm, n_chunks, fc, conv_w, final_norm):
    i = pl.program_id(1)
    last = pl.num_programs(1) - 1
    g = g_ref[...]
    x = xm_ref[0]
    hp = jnp.where(i > 0, _rmsnorm(xp_ref[0], g), 0.0)
    hn = jnp.where(i < last, _rmsnorm(xn_ref[0], g), 0.0)
    h_ref[0:HALO, :] = hp.astype(BF16)
    h_ref[HALO:HALO + tm, :] = _rmsnorm(x, g).astype(BF16)
    h_ref[HALO + tm:, :] = hn.astype(BF16)
    hb = h_ref[...]
    pad = conv_w // 2
    for c in range(n_chunks):
        u = jnp.dot(hb, wup_ref[c], preferred_element_type=F32)
        w = cw_ref[c]
        y = cb_ref[c]
        for j in range(conv_w):
            off = HALO - pad + j
            y = y + u[off:off + tm] * w[j:j + 1]
        a = y[:, :fc]
        val = y[:, fc:]
        act_ref[:, c * fc:(c + 1) * fc] = (a * jax.nn.sigmoid(a) * val).astype(BF16)
    r = x + jnp.dot(act_ref[...], wdn_ref[...], preferred_element_type=F32)
    if final_norm:
        r = _rmsnorm(r, fg_ref[...])
    o_ref[0] = r


def _conv_ffn(x, norm_g, w_up, conv_w, conv_b, w_down, final_g, *, final_norm):
    b, l, d = x.shape
    d_ff = w_down.shape[0]
    cw = conv_w.shape[0]
    fc = V7X_MXU_DIM
    n_chunks = d_ff // fc
    assert n_chunks * fc == d_ff and cw // 2 <= HALO
    tm = min(512, l)
    assert l % tm == 0 and tm % HALO == 0
    nt = l // tm
    rb = tm // HALO
    wup = w_up.reshape(d, 2, n_chunks, fc).transpose(2, 0, 1, 3).reshape(n_chunks, d, 2 * fc).astype(BF16)
    cwr = conv_w.reshape(cw, 2, n_chunks, fc).transpose(2, 0, 1, 3).reshape(n_chunks, cw, 2 * fc)
    cbr = conv_b.reshape(2, n_chunks, fc).transpose(1, 0, 2).reshape(n_chunks, 1, 2 * fc)
    wdn = w_down.astype(BF16)
    kern = functools.partial(_ffn_kernel, tm=tm, n_chunks=n_chunks, fc=fc, conv_w=cw, final_norm=final_norm)
    vmem = (wup.size * 2 + wdn.size * 2 + 4 * tm * d * 4 + (tm + 2 * HALO) * d * 2 + tm * d_ff * 2
            + 6 * (tm + 2 * HALO) * 2 * fc * 4 + 4 * tm * d * 4)
    return pl.pallas_call(
        kern,
        grid=(b, nt),
        in_specs=[
            pl.BlockSpec((1, tm, d), lambda bi, i: (bi, i, 0)),
            pl.BlockSpec((1, HALO, d), lambda bi, i: (bi, jnp.maximum(i * rb - 1, 0), 0)),
            pl.BlockSpec((1, HALO, d), lambda bi, i: (bi, jnp.minimum((i + 1) * rb, l // HALO - 1), 0)),
            _resident((1, d)),
            _resident(wup.shape),
            _resident(cwr.shape),
            _resident(cbr.shape),
            _resident(wdn.shape),
            _resident((1, d)),
        ],
        out_specs=pl.BlockSpec((1, tm, d), lambda bi, i: (bi, i, 0)),
        out_shape=jax.ShapeDtypeStruct((b, l, d), F32),
        scratch_shapes=[pltpu.VMEM((tm + 2 * HALO, d), BF16), pltpu.VMEM((tm, d_ff), BF16)],
        compiler_params=pltpu.CompilerParams(
            dimension_semantics=("parallel", "arbitrary"), vmem_limit_bytes=_vmem_limit(vmem)),
        name="conv_ffn",
    )(x, x, x, norm_g.reshape(1, d), wup, cwr, cbr, wdn, final_g.reshape(1, d))


def _split3(x):
    hi = x.astype(BF16)
    r = x - hi.astype(F32)
    mid = r.astype(BF16)
    lo = (r - mid.astype(F32)).astype(BF16)
    return hi, mid, lo


def _proj_kernel(x_ref, g_ref, w_ref, wg_ref, bg_ref, pre_ref, suf_ref,
                 q_ref, k_ref, qt_ref, kt_ref, v_ref, o_ref, gc_ref, gr_ref, *, mq, mv, heads, k_scale):
    h = _rmsnorm(x_ref[0], g_ref[...]).astype(BF16)
    q = jnp.dot(h, w_ref[:, 0:mq], preferred_element_type=F32)
    k = jnp.dot(h, w_ref[:, mq:2 * mq], preferred_element_type=F32) * k_scale
    q_ref[0] = q.astype(BF16)
    k_ref[0] = k.astype(BF16)
    qt_ref[0] = q.T.astype(BF16)
    kt_ref[0] = k.T.astype(BF16)
    v_ref[0] = jnp.dot(h, w_ref[:, 2 * mq:2 * mq + mv], preferred_element_type=F32).astype(BF16)
    o_ref[0] = jnp.dot(h, w_ref[:, 2 * mq + mv:], preferred_element_type=F32).astype(BF16)
    gt = jnp.dot(h, wg_ref[...], preferred_element_type=F32) + bg_ref[...]
    lane = lax.broadcasted_iota(jnp.int32, gt.shape, 1)
    kind = lane // heads
    log_sig = jnp.minimum(gt, 0.0) - jnp.log1p(jnp.exp(-jnp.abs(gt)))
    parts = _split3(log_sig)
    t = pre_ref.shape[0]
    pre, suf = [], []
    for c in range(gt.shape[0] // t):
        rows = slice(c * t, (c + 1) * t)
        pre.append(sum(jnp.dot(pre_ref[...], p[rows], preferred_element_type=F32) for p in parts))
        suf.append(sum(jnp.dot(suf_ref[...], p[rows], preferred_element_type=F32) for p in parts))
    gt = jnp.where(kind == 1, jnp.concatenate(pre, axis=0), jnp.where(kind == 3, jnp.concatenate(suf, axis=0), gt))
    gc_ref[0] = gt
    gr_ref[0] = gt.T[0:gr_ref.shape[1], :]


def _mlstm_proj(x, norm_g, w_in, b_gate, *, heads, mv):
    b, l, d = x.shape
    ng = N_GATE_KINDS * heads
    mq = (w_in.shape[1] - 2 * mv - ng) // 2
    dqk = mq // heads
    tm = min(512, l)
    assert l % tm == 0 and ng <= V7X_LANES
    w_main = w_in[:, :2 * mq + 2 * mv].astype(BF16)
    w_gate = jnp.pad(w_in[:, 2 * mq + 2 * mv:], ((0, 0), (0, V7X_LANES - ng))).astype(BF16)
    b_pad = jnp.pad(b_gate, (0, V7X_LANES - ng)).reshape(1, V7X_LANES)
    t = min(SCAN_CHUNK, l)
    assert tm % t == 0
    pos = np.arange(t)
    pre_m = jnp.asarray(pos[None, :] <= pos[:, None], BF16)
    suf_m = jnp.asarray(pos[None, :] >= pos[:, None], BF16)
    kern = functools.partial(_proj_kernel, mq=mq, mv=mv, heads=heads, k_scale=float(dqk) ** -0.5)
    row = lambda bi, i: (bi, i, 0)
    col = lambda bi, i: (bi, 0, i)
    vmem = (w_main.size * 2 + w_gate.size * 2 + 2 * tm * d * 4 + 2 * tm * (4 * mq + 2 * mv) * 2 + 8 * tm * mv * 4)
    return pl.pallas_call(
        kern,
        grid=(b, l // tm),
        in_specs=[
            pl.BlockSpec((1, tm, d), row),
            _resident((1, d)),
            _resident(w_main.shape),
            _resident(w_gate.shape),
            _resident((1, V7X_LANES)),
            _resident((t, t)),
            _resident((t, t)),
        ],
        out_specs=[
            pl.BlockSpec((1, tm, mq), row),
            pl.BlockSpec((1, tm, mq), row),
            pl.BlockSpec((1, mq, tm), col),
            pl.BlockSpec((1, mq, tm), col),
            pl.BlockSpec((1, tm, mv), row),
            pl.BlockSpec((1, tm, mv), row),
            pl.BlockSpec((1, tm, V7X_LANES), row),
            pl.BlockSpec((1, ng, tm), col),
        ],
        out_shape=[
            jax.ShapeDtypeStruct((b, l, mq), BF16),
            jax.ShapeDtypeStruct((b, l, mq), BF16),
            jax.ShapeDtypeStruct((b, mq, l), BF16),
            jax.ShapeDtypeStruct((b, mq, l), BF16),
            jax.ShapeDtypeStruct((b, l, mv), BF16),
            jax.ShapeDtypeStruct((b, l, mv), BF16),
            jax.ShapeDtypeStruct((b, l, V7X_LANES), F32),
            jax.ShapeDtypeStruct((b, ng, l), F32),
        ],
        compiler_params=pltpu.CompilerParams(
            dimension_semantics=("parallel", "parallel"), vmem_limit_bytes=_vmem_limit(vmem)),
        name="mlstm_proj",
    )(x, norm_g.reshape(1, d), w_main, w_gate, b_pad, pre_m, suf_m)


def _scan_kernel(qf_ref, kf_ref, qtf_ref, ktf_ref, vf_ref, gcf_ref, grf_ref,
                 qb_ref, kb_ref, qtb_ref, ktb_ref, vb_ref, gcb_ref, grb_ref,
                 hf_ref, hb_ref, c_ref, rhs_ref, n_ref, m_ref, *, t, heads, dqk, dv):
    step = pl.program_id(1)
    par = step % 2

    @pl.when(step == 0)
    def _():
        c_ref[...] = jnp.zeros_like(c_ref)
        rhs_ref[...] = jnp.zeros_like(rhs_ref)
        n_ref[...] = jnp.zeros_like(n_ref)
        m_ref[...] = jnp.zeros_like(m_ref)

    src_i = lax.broadcasted_iota(jnp.int32, (t, t), 0)
    tgt_i = lax.broadcasted_iota(jnp.int32, (t, t), 1)
    first_head = lax.broadcasted_iota(jnp.int32, (t, 2 * dqk), 1) < dqk
    n_row = lax.broadcasted_iota(jnp.int32, (N_ROWS, 2 * t), 0)
    n_lane_first = lax.broadcasted_iota(jnp.int32, (N_ROWS, 2 * t), 1) < t
    own_block = ((n_row == 0) & n_lane_first) | ((n_row == 1) & jnp.logical_not(n_lane_first))
    n_row_q = lax.broadcasted_iota(jnp.int32, (N_ROWS, 2 * dqk), 0)
    streams = (
        (qf_ref, kf_ref, qtf_ref, ktf_ref, vf_ref, gcf_ref, grf_ref, hf_ref, src_i <= tgt_i, 0, t - 1),
        (qb_ref, kb_ref, qtb_ref, ktb_ref, vb_ref, gcb_ref, grb_ref, hb_ref, src_i >= tgt_i, 2 * heads, 0),
    )
    dn_nt = (((1,), (1,)), ((), ()))
    dn_tn = (((0,), (0,)), ((), ()))
    half = heads // 2

    def block_diag(a2):
        zero = jnp.zeros_like(a2)
        return jnp.concatenate([jnp.where(first_head, a2, zero), jnp.where(first_head, zero, a2)], axis=0)

    every = [(di, h, di * heads + h) for di in range(len(streams)) for h in range(heads)]
    pairs = [(di, p, di * half + p) for di in range(len(streams)) for p in range(half)]
    st = {}
    for di, h, idx in every:
        gc_ref, gr_ref, base, end = streams[di][5], streams[di][6], streams[di][9], streams[di][10]
        ig_r = gr_ref[0, base + h:base + h + 1, :]
        b_r = gr_ref[0, base + heads + h:base + heads + h + 1, :]
        total = b_r[:, end:end + 1]
        m_prev = m_ref[idx]
        st[idx] = dict(b_r=b_r, inter=b_r + m_prev, gk=total - b_r + ig_r, carry=total + m_prev,
                       src_c=gc_ref[0, :, base + h:base + h + 1] - gc_ref[0, :, base + heads + h:base + heads + h + 1])
    for di, h, idx in every:
        e = st[idx]
        m_new = jnp.maximum(e["carry"], jnp.max(e["gk"], axis=-1, keepdims=True))
        m_ref[idx] = m_new
        e["wk"] = jnp.exp(e["gk"] - m_new)
        e["decay"] = jnp.exp(e["carry"] - m_new)

    sp = {}
    for di, p, pi in pairs:
        sl = slice(2 * p * dqk, (2 * p + 2) * dqk)
        sp[pi] = dict(k2=streams[di][1][0, :, sl], n_prev=n_ref[pi])
        lhs = jnp.concatenate([sp[pi]["k2"], sp[pi]["n_prev"].astype(BF16)], axis=0)
        s2 = lax.dot_general(lhs, block_diag(streams[di][0][0, :, sl]), dn_nt, preferred_element_type=F32)
        for j in range(2):
            e = st[di * heads + 2 * p + j]
            e["qk"] = s2[:t, j * t:(j + 1) * t]
            e["qn"] = s2[t + j:t + j + 1, j * t:(j + 1) * t]

    for di, h, idx in every:
        e = st[idx]
        e["v"] = streams[di][4][0, :, h * dv:(h + 1) * dv]
        e["kwt"] = (streams[di][3][0, h * dqk:(h + 1) * dqk, :].astype(F32) * e["wk"]).astype(BF16)
    for di, h, idx in every:
        e = st[idx]
        e["upd"] = jnp.dot(e["kwt"], e["v"], preferred_element_type=F32)
        rhs_ref[par, idx, :t, :] = e["v"]
    for di, p, pi in pairs:
        ea, eb = st[di * heads + 2 * p], st[di * heads + 2 * p + 1]
        wk2 = jnp.concatenate([ea["wk"], eb["wk"]], axis=1)
        w_rows = jnp.where(own_block, wk2, 0.0).astype(BF16)
        dec = jnp.where(n_row_q == 0, ea["decay"], eb["decay"])
        n_ref[pi] = dec * sp[pi]["n_prev"] + jnp.dot(w_rows, block_diag(sp[pi]["k2"]), preferred_element_type=F32)
    for di, h, idx in every:
        e = st[idx]
        c_new = e["decay"] * c_ref[idx] + e["upd"]
        c_ref[idx] = c_new
        rhs_ref[1 - par, idx, t:, :] = c_new.astype(BF16)

    for di, h, idx in every:
        e = st[idx]
        e["dlog"] = jnp.where(streams[di][8], e["src_c"] + e["b_r"], -jnp.inf)
    for di, h, idx in every:
        e = st[idx]
        e["m_t"] = jnp.maximum(e["inter"], jnp.max(e["dlog"], axis=0, keepdims=True))
    for di, h, idx in every:
        e = st[idx]
        e["s"] = e["qk"] * jnp.exp(e["dlog"] - e["m_t"])
        e["sc"] = jnp.exp(e["inter"] - e["m_t"])
    for di, h, idx in every:
        e = st[idx]
        den = jnp.sum(e["s"], axis=0, keepdims=True) + e["sc"] * e["qn"]
        e["r"] = 1.0 / jnp.maximum(jnp.abs(den), jnp.exp(-e["m_t"]))
    for di, h, idx in every:
        e = st[idx]
        qt32 = streams[di][2][0, h * dqk:(h + 1) * dqk, :].astype(F32)
        e["lhs_t"] = jnp.concatenate(
            [(e["s"] * e["r"]).astype(BF16), (qt32 * (e["sc"] * e["r"])).astype(BF16)], axis=0)
    for di, h, idx in every:
        streams[di][7][0, :, h * dv:(h + 1) * dv] = lax.dot_general(
            st[idx]["lhs_t"], rhs_ref[par, idx], dn_tn, preferred_element_type=F32).astype(streams[di][7].dtype)


def _mlstm_scan(q, k, qt, kt, v, gc, gr, *, heads):
    b, l, mq = q.shape
    mv = v.shape[-1]
    dqk, dv = mq // heads, mv // heads
    t = min(SCAN_CHUNK, l)
    assert l % t == 0 and heads % 2 == 0
    nc = l // t
    ng = gr.shape[1]
    kern = functools.partial(_scan_kernel, t=t, heads=heads, dqk=dqk, dv=dv)
    fwd = lambda bi, c: (bi, c, 0)
    bwd = lambda bi, c: (bi, nc - 1 - c, 0)
    fwd_r = lambda bi, c: (bi, 0, c)
    bwd_r = lambda bi, c: (bi, 0, nc - 1 - c)

    def stream(tok, rowm):
        return [pl.BlockSpec((1, t, mq), tok), pl.BlockSpec((1, t, mq), tok),
                pl.BlockSpec((1, mq, t), rowm), pl.BlockSpec((1, mq, t), rowm), pl.BlockSpec((1, t, mv), tok),
                pl.BlockSpec((1, t, V7X_LANES), tok), pl.BlockSpec((1, ng, t), rowm)]

    return pl.pallas_call(
        kern,
        grid=(b, nc),
        in_specs=stream(fwd, fwd_r) + stream(bwd, bwd_r),
        out_specs=[pl.BlockSpec((1, t, mv), fwd), pl.BlockSpec((1, t, mv), bwd)],
        out_shape=[jax.ShapeDtypeStruct((b, l, mv), BF16), jax.ShapeDtypeStruct((b, l, mv), BF16)],
        scratch_shapes=[pltpu.VMEM((2 * heads, dqk, dv), F32), pltpu.VMEM((2, 2 * heads, t + dqk, dv), BF16),
                        pltpu.VMEM((heads, N_ROWS, 2 * dqk), F32), pltpu.VMEM((2 * heads, 1, 1), F32)],
        compiler_params=pltpu.CompilerParams(dimension_semantics=("parallel", "arbitrary")),
        name="mlstm_scan",
    )(q, k, qt, kt, v, gc, gr, q, k, qt, kt, v, gc, gr)


def _mout_kernel(hf_ref, hb_ref, o_ref, x_ref, hg_ref, w_ref, out_ref, y_ref, *, heads, dv):
    for h in range(heads):
        sl = slice(h * dv, (h + 1) * dv)
        hs = hf_ref[0, :, sl].astype(F32) + hb_ref[0, :, sl].astype(F32)
        hs = hs * lax.rsqrt(jnp.mean(hs * hs, axis=-1, keepdims=True) + EPS)
        hs = hs * hg_ref[:, sl]
        y_ref[:, sl] = (hs * jax.nn.sigmoid(o_ref[0, :, sl].astype(F32))).astype(BF16)
    out_ref[0] = x_ref[0] + jnp.dot(y_ref[...], w_ref[...], preferred_element_type=F32)


def _mlstm_out(hf, hb, o, x, head_g, w_out):
    b, l, d = x.shape
    heads, dv = head_g.shape
    mv = heads * dv
    tm = min(512, l)
    assert l % tm == 0
    row = lambda bi, i: (bi, i, 0)
    w = w_out.astype(BF16)
    vmem = w.size * 2 + 2 * tm * (3 * mv * 2 + 2 * d * 4) + tm * mv * 2 + 6 * tm * d * 4
    return pl.pallas_call(
        functools.partial(_mout_kernel, heads=heads, dv=dv),
        grid=(b, l // tm),
        in_specs=[pl.BlockSpec((1, tm, mv), row), pl.BlockSpec((1, tm, mv), row), pl.BlockSpec((1, tm, mv), row),
                  pl.BlockSpec((1, tm, d), row), _resident((1, mv)), _resident(w.shape)],
        out_specs=pl.BlockSpec((1, tm, d), row),
        out_shape=jax.ShapeDtypeStruct((b, l, d), F32),
        scratch_shapes=[pltpu.VMEM((tm, mv), BF16)],
        compiler_params=pltpu.CompilerParams(
            dimension_semantics=("parallel", "parallel"), vmem_limit_bytes=_vmem_limit(vmem)),
        name="mlstm_out",
    )(hf, hb, o, x, head_g.reshape(1, mv), w)


def _dft_tables(l1, l2, gd, n2h, n2b):
    l = l1 * l2
    a1 = 2.0 * np.pi * np.outer(np.arange(l1), np.arange(l1)) / l1
    wa = np.stack([np.cos(a1), -np.sin(a1)]) / np.sqrt(l1)
    wa = np.einsum("rkn,ij->krinj", wa, np.eye(n2h)).reshape(l1 * 2 * n2h, l1 * n2h)
    tw = 2.0 * np.pi * np.einsum("k,jhi->jhki", np.arange(l1),
                                 np.arange(l2).reshape(l2 // n2b, n2b // n2h, n2h)) / l
    twr, twi = np.cos(tw)[..., None], -np.sin(tw)[..., None]
    a2 = 2.0 * np.pi * np.outer(np.arange(l2), np.arange(l2)) / l2
    c2, s2 = np.cos(a2), np.sin(a2)
    wc = np.block([[c2, s2], [-s2, c2]]) / np.sqrt(l2)
    ac = 2.0 * np.pi * np.outer(np.arange(gd), np.arange(gd)) / gd
    cs = np.stack([np.cos(ac), np.sin(ac)]) / np.sqrt(gd)
    return (jnp.asarray(wa, BF16), jnp.asarray(twr, F32), jnp.asarray(twi, F32),
            jnp.asarray(wc, BF16), jnp.asarray(cs, BF16))


def _fa_kernel(x_ref, g_ref, wa_ref, twr_ref, twi_ref, out_ref, *, l1, n2h, n2b, d):
    g = g_ref[...]
    wa = wa_ref[...]
    halves = []
    for hh in range(n2b // n2h):
        x = x_ref[0, :, hh * n2h:(hh + 1) * n2h, :].reshape(l1 * n2h, d)
        a = jnp.dot(wa, _rmsnorm(x, g).astype(BF16), preferred_element_type=F32)
        a = a.reshape(l1, 2, n2h, d)
        ar = a[:, 0]
        ai = a[:, 1]
        tr = twr_ref[0, hh]
        ti = twi_ref[0, hh]
        halves.append(jnp.stack([ar * tr - ai * ti, ar * ti + ai * tr], axis=1))
    out_ref[0] = jnp.concatenate(halves, axis=2).astype(BF16)


def _fc_kernel(ba_ref, x_ref, wc_ref, cs_ref, wout_ref, bout_ref, out_ref, z_ref, f_ref, *, l2, k1b, d, gd):
    wc = wc_ref[...]
    for j in range(k1b):
        z = jnp.dot(wc, ba_ref[0, j], preferred_element_type=F32)
        z_ref[0, j * l2:(j + 1) * l2, :] = z[:l2].astype(BF16)
        z_ref[1, j * l2:(j + 1) * l2, :] = z[l2:].astype(BF16)
    for gi in range(d // gd):
        sl = slice(gi * gd, (gi + 1) * gd)
        f = (jnp.dot(z_ref[0, :, sl], cs_ref[0], preferred_element_type=F32)
             + jnp.dot(z_ref[1, :, sl], cs_ref[1], preferred_element_type=F32))
        f_ref[:, sl] = f.astype(BF16)
    y = jnp.dot(f_ref[...], wout_ref[...], preferred_element_type=F32) + bout_ref[...]
    for j in range(k1b):
        out_ref[0, :, j, :] = x_ref[0, :, j, :] + y[j * l2:(j + 1) * l2]


def _fourier_mixer(x, norm_g, w_out, b_out):
    b, l, d = x.shape
    l2 = min(DFT_INNER, l)
    l1 = l // l2
    assert l1 * l2 == l and d % F_GROUPS == 0
    gd = d // F_GROUPS
    n2h = V7X_SUBLANES
    n2b = min(V7X_BF16_ROWS, l2)
    k1b = min(V7X_SUBLANES, l1)
    assert l2 % n2b == 0 and n2b % n2h == 0 and l1 % k1b == 0
    wa, twr, twi, wc, cs = _dft_tables(l1, l2, gd, n2h, n2b)

    tw_spec = pl.BlockSpec((1, n2b // n2h, l1, n2h, 1), lambda bi, j: (j, 0, 0, 0, 0))
    ba = pl.pallas_call(
        functools.partial(_fa_kernel, l1=l1, n2h=n2h, n2b=n2b, d=d),
        grid=(b, l2 // n2b),
        in_specs=[pl.BlockSpec((1, l1, n2b, d), lambda bi, j: (bi, 0, j, 0)),
                  _resident((1, d)), _resident(wa.shape), tw_spec, tw_spec],
        out_specs=pl.BlockSpec((1, l1, 2, n2b, d), lambda bi, j: (bi, 0, 0, j, 0)),
        out_shape=jax.ShapeDtypeStruct((b, l1, 2, l2, d), BF16),
        compiler_params=pltpu.CompilerParams(
            dimension_semantics=("parallel", "parallel"),
            vmem_limit_bytes=_vmem_limit(l1 * n2b * d * (2 * 4 + 2 * 2 * 2 + 6 * 4) + wa.size * 2)),
        name="fourier_seq_a",
    )(x.reshape(b, l1, l2, d), norm_g.reshape(1, d), wa, twr, twi)

    rows = k1b * l2
    out = pl.pallas_call(
        functools.partial(_fc_kernel, l2=l2, k1b=k1b, d=d, gd=gd),
        grid=(b, l1 // k1b),
        in_specs=[pl.BlockSpec((1, k1b, 2 * l2, d), lambda bi, j: (bi, j, 0, 0)),
                  pl.BlockSpec((1, l2, k1b, d), lambda bi, j: (bi, 0, j, 0)),
                  _resident(wc.shape), _resident(cs.shape), _resident((d, d)), _resident((1, d))],
        out_specs=pl.BlockSpec((1, l2, k1b, d), lambda bi, j: (bi, 0, j, 0)),
        out_shape=jax.ShapeDtypeStruct((b, l2, l1, d), F32),
        scratch_shapes=[pltpu.VMEM((2, rows, d), BF16), pltpu.VMEM((rows, d), BF16)],
        compiler_params=pltpu.CompilerParams(
            dimension_semantics=("parallel", "parallel"),
            vmem_limit_bytes=_vmem_limit(rows * d * (2 * 2 * 2 + 4 * 4 + 3 * 2 + 4 * 4) + 4 * d * d)),
        name="fourier_seq_c",
    )(ba.reshape(b, l1, 2 * l2, d), x.reshape(b, l2, l1, d), wc, cs, w_out.astype(BF16), b_out.reshape(1, d))
    return out.reshape(b, l, d)


def _mlstm_mixer(x, norm_g, w_in, b_gate, head_g, w_out):
    heads, dv = head_g.shape
    q, k, qt, kt, v, o, gc, gr = _mlstm_proj(x, norm_g, w_in, b_gate, heads=heads, mv=heads * dv)
    hf, hb = _mlstm_scan(q, k, qt, kt, v, gc, gr, heads=heads)
    return _mlstm_out(hf, hb, o, x, head_g, w_out)


def _trunk(x, mix_norm_g, m_w_in, m_b_gate, m_head_g, m_w_out, f_w_out, f_b_out,
           ffn_norm_g, ffn_w_up, ffn_conv_w, ffn_conv_b, ffn_w_down, final_norm_g):
    depth = mix_norm_g.shape[0]
    n_mixers = 2
    for i in range(depth):
        j = i // n_mixers
        if i % n_mixers == 0:
            x = _mlstm_mixer(x, mix_norm_g[i], m_w_in[j], m_b_gate[j], m_head_g[j], m_w_out[j])
        else:
            x = _fourier_mixer(x, mix_norm_g[i], f_w_out[j], f_b_out[j])
        x = _conv_ffn(x, ffn_norm_g[i], ffn_w_up[i], ffn_conv_w[i], ffn_conv_b[i], ffn_w_down[i],
                      final_norm_g, final_norm=(i == depth - 1))
    return x


def kernel(x_prompt, x_sample, mix_norm_g, m_w_in, m_b_gate, m_head_g, m_w_out, f_w_out, f_b_out,
           ffn_norm_g, ffn_w_up, ffn_conv_w, ffn_conv_b, ffn_w_down, final_norm_g):
    params = (mix_norm_g, m_w_in, m_b_gate, m_head_g, m_w_out, f_w_out, f_b_out,
              ffn_norm_g, ffn_w_up, ffn_conv_w, ffn_conv_b, ffn_w_down, final_norm_g)
    return (_trunk(x_prompt, *params), _trunk(x_sample, *params))
```

```python
import functools

import numpy as np
import jax
import jax.numpy as jnp
from jax import lax
from jax.experimental import pallas as pl
from jax.experimental.pallas import tpu as pltpu

F32 = jnp.float32
BF16 = jnp.bfloat16

EPS = 1e-6
F_GROUPS = 4
N_GATE_KINDS = 4

V7X_VMEM_BYTES = 64 * 1024 * 1024
V7X_LANES = 128
V7X_SUBLANES = 8
V7X_BF16_ROWS = 16
V7X_MXU_DIM = 256

HALO = V7X_SUBLANES
SCAN_CHUNK = 128
N_ROWS = V7X_BF16_ROWS
DFT_INNER = 128


def _vmem_limit(nbytes):
    return int(min(max(nbytes, 32 * 1024 * 1024), V7X_VMEM_BYTES - 8 * 1024 * 1024))


def _rmsnorm(x, g):
    y = x * lax.rsqrt(jnp.mean(x * x, axis=-1, keepdims=True) + EPS)
    return y * g


def _resident(shape):
    nd = len(shape)
    return pl.BlockSpec(shape, lambda *_: (0,) * nd, pipeline_mode=pl.Buffered(1))


def _ffn_kernel(xm_ref, xp_ref, xn_ref, g_ref, wup_ref, cw_ref, cb_ref, wdn_ref, fg_ref,
                o_ref, h_ref, act_ref, *, tm, n_chunks, fc, conv_w, final_norm):
    i = pl.program_id(1)
    last = pl.num_programs(1) - 1
    g = g_ref[...]
    x = xm_ref[0]
    hp = jnp.where(i > 0, _rmsnorm(xp_ref[0], g), 0.0)
    hn = jnp.where(i < last, _rmsnorm(xn_ref[0], g), 0.0)
    h_ref[0:HALO, :] = hp.astype(BF16)
    h_ref[HALO:HALO + tm, :] = _rmsnorm(x, g).astype(BF16)
    h_ref[HALO + tm:, :] = hn.astype(BF16)
    hb = h_ref[...]
    pad = conv_w // 2
    d_ff = n_chunks * fc

    def conv_cols(lo):
        u = jnp.dot(hb, wup_ref[:, lo:lo + fc], preferred_element_type=F32)
        y = cb_ref[:, lo:lo + fc]
        for j in range(conv_w):
            off = HALO - pad + j
            y = y + u[off:off + tm] * cw_ref[j:j + 1, lo:lo + fc]
        return y

    for c in range(n_chunks):
        a = conv_cols(c * fc)
        val = conv_cols(d_ff + c * fc)
        act_ref[:, c * fc:(c + 1) * fc] = (a * jax.nn.sigmoid(a) * val).astype(BF16)
    r = x + jnp.dot(act_ref[...], wdn_ref[...], preferred_element_type=F32)
    if final_norm:
        r = _rmsnorm(r, fg_ref[...])
    o_ref[0] = r


def _conv_ffn(x, norm_g, w_up, conv_w, conv_b, w_down, final_g, *, final_norm):
    b, l, d = x.shape
    d_ff = w_down.shape[0]
    cw = conv_w.shape[0]
    fc = V7X_MXU_DIM
    n_chunks = d_ff // fc
    assert n_chunks * fc == d_ff and cw // 2 <= HALO
    tm = min(1024, l)
    assert l % tm == 0 and tm % HALO == 0
    nt = l // tm
    rb = tm // HALO
    assert (2 * d_ff) % V7X_LANES == 0 and fc % V7X_LANES == 0
    wup = w_up.astype(BF16)
    cwr = conv_w
    cbr = conv_b.reshape(1, 2 * d_ff)
    wdn = w_down.astype(BF16)
    kern = functools.partial(_ffn_kernel, tm=tm, n_chunks=n_chunks, fc=fc, conv_w=cw, final_norm=final_norm)
    vmem = (wup.size * 2 + wdn.size * 2 + 4 * tm * d * 4 + (tm + 2 * HALO) * d * 2 + tm * d_ff * 2
            + 8 * (tm + 2 * HALO) * fc * 4 + 3 * tm * d * 4)
    return pl.pallas_call(
        kern,
        grid=(b, nt),
        in_specs=[
            pl.BlockSpec((1, tm, d), lambda bi, i: (bi, i, 0)),
            pl.BlockSpec((1, HALO, d), lambda bi, i: (bi, jnp.maximum(i * rb - 1, 0), 0)),
            pl.BlockSpec((1, HALO, d), lambda bi, i: (bi, jnp.minimum((i + 1) * rb, l // HALO - 1), 0)),
            _resident((1, d)),
            _resident(wup.shape),
            _resident(cwr.shape),
            _resident(cbr.shape),
            _resident(wdn.shape),
            _resident((1, d)),
        ],
        out_specs=pl.BlockSpec((1, tm, d), lambda bi, i: (bi, i, 0)),
        out_shape=jax.ShapeDtypeStruct((b, l, d), F32),
        scratch_shapes=[pltpu.VMEM((tm + 2 * HALO, d), BF16), pltpu.VMEM((tm, d_ff), BF16)],
        compiler_params=pltpu.CompilerParams(
            dimension_semantics=("parallel", "arbitrary"), vmem_limit_bytes=_vmem_limit(vmem)),
        name="conv_ffn",
    )(x, x, x, norm_g.reshape(1, d), wup, cwr, cbr, wdn, final_g.reshape(1, d))


def _split3(x):
    hi = x.astype(BF16)
    r = x - hi.astype(F32)
    mid = r.astype(BF16)
    lo = (r - mid.astype(F32)).astype(BF16)
    return hi, mid, lo


def _proj_kernel(x_ref, g_ref, w_ref, wg_ref, bg_ref, pre_ref, suf_ref,
                 q_ref, k_ref, qt_ref, kt_ref, v_ref, o_ref, gc_ref, gr_ref, *, mq, mv, heads, k_scale):
    h = _rmsnorm(x_ref[0], g_ref[...]).astype(BF16)
    q = jnp.dot(h, w_ref[:, 0:mq], preferred_element_type=F32)
    k = jnp.dot(h, w_ref[:, mq:2 * mq], preferred_element_type=F32) * k_scale
    q_ref[0] = q.astype(BF16)
    k_ref[0] = k.astype(BF16)
    qt_ref[0] = q.T.astype(BF16)
    kt_ref[0] = k.T.astype(BF16)
    v_ref[0] = jnp.dot(h, w_ref[:, 2 * mq:2 * mq + mv], preferred_element_type=F32).astype(BF16)
    o_ref[0] = jnp.dot(h, w_ref[:, 2 * mq + mv:], preferred_element_type=F32).astype(BF16)
    gt = jnp.dot(h, wg_ref[...], preferred_element_type=F32) + bg_ref[...]
    lane = lax.broadcasted_iota(jnp.int32, gt.shape, 1)
    kind = lane // heads
    log_sig = jnp.minimum(gt, 0.0) - jnp.log1p(jnp.exp(-jnp.abs(gt)))
    parts = _split3(log_sig)
    t = pre_ref.shape[0]
    pre, suf = [], []
    for c in range(gt.shape[0] // t):
        rows = slice(c * t, (c + 1) * t)
        pre.append(sum(jnp.dot(pre_ref[...], p[rows], preferred_element_type=F32) for p in parts))
        suf.append(sum(jnp.dot(suf_ref[...], p[rows], preferred_element_type=F32) for p in parts))
    gt = jnp.where(kind == 1, jnp.concatenate(pre, axis=0), jnp.where(kind == 3, jnp.concatenate(suf, axis=0), gt))
    gc_ref[0] = gt
    gr_ref[0] = gt.T[0:gr_ref.shape[1], :]


def _mlstm_proj(x, norm_g, w_in, b_gate, *, heads, mv):
    b, l, d = x.shape
    ng = N_GATE_KINDS * heads
    mq = (w_in.shape[1] - 2 * mv - ng) // 2
    dqk = mq // heads
    tm = min(1024, l)
    assert l % tm == 0 and ng <= V7X_LANES
    w_main = w_in[:, :2 * mq + 2 * mv].astype(BF16)
    w_gate = jnp.pad(w_in[:, 2 * mq + 2 * mv:], ((0, 0), (0, V7X_LANES - ng))).astype(BF16)
    b_pad = jnp.pad(b_gate, (0, V7X_LANES - ng)).reshape(1, V7X_LANES)
    t = min(SCAN_CHUNK, l)
    assert tm % t == 0
    pos = np.arange(t)
    pre_m = jnp.asarray(pos[None, :] <= pos[:, None], BF16)
    suf_m = jnp.asarray(pos[None, :] >= pos[:, None], BF16)
    kern = functools.partial(_proj_kernel, mq=mq, mv=mv, heads=heads, k_scale=float(dqk) ** -0.5)
    row = lambda bi, i: (bi, i, 0)
    col = lambda bi, i: (bi, 0, i)
    vmem = (w_main.size * 2 + w_gate.size * 2 + 2 * tm * d * 4 + 2 * tm * (4 * mq + 2 * mv) * 2 + 8 * tm * mv * 4)
    return pl.pallas_call(
        kern,
        grid=(b, l // tm),
        in_specs=[
            pl.BlockSpec((1, tm, d), row),
            _resident((1, d)),
            _resident(w_main.shape),
            _resident(w_gate.shape),
            _resident((1, V7X_LANES)),
            _resident((t, t)),
            _resident((t, t)),
        ],
        out_specs=[
            pl.BlockSpec((1, tm, mq), row),
            pl.BlockSpec((1, tm, mq), row),
            pl.BlockSpec((1, mq, tm), col),
            pl.BlockSpec((1, mq, tm), col),
            pl.BlockSpec((1, tm, mv), row),
            pl.BlockSpec((1, tm, mv), row),
            pl.BlockSpec((1, tm, V7X_LANES), row),
            pl.BlockSpec((1, ng, tm), col),
        ],
        out_shape=[
            jax.ShapeDtypeStruct((b, l, mq), BF16),
            jax.ShapeDtypeStruct((b, l, mq), BF16),
            jax.ShapeDtypeStruct((b, mq, l), BF16),
            jax.ShapeDtypeStruct((b, mq, l), BF16),
            jax.ShapeDtypeStruct((b, l, mv), BF16),
            jax.ShapeDtypeStruct((b, l, mv), BF16),
            jax.ShapeDtypeStruct((b, l, V7X_LANES), F32),
            jax.ShapeDtypeStruct((b, ng, l), F32),
        ],
        compiler_params=pltpu.CompilerParams(
            dimension_semantics=("parallel", "parallel"), vmem_limit_bytes=_vmem_limit(vmem)),
        name="mlstm_proj",
    )(x, norm_g.reshape(1, d), w_main, w_gate, b_pad, pre_m, suf_m)


def _scan_kernel(qf_ref, kf_ref, qtf_ref, ktf_ref, vf_ref, gcf_ref, grf_ref,
                 qb_ref, kb_ref, qtb_ref, ktb_ref, vb_ref, gcb_ref, grb_ref,
                 hf_ref, hb_ref, c_ref, rhs_ref, n_ref, m_ref, *, t, heads, dqk, dv):
    step = pl.program_id(1)
    par = step % 2

    @pl.when(step == 0)
    def _():
        c_ref[...] = jnp.zeros_like(c_ref)
        rhs_ref[...] = jnp.zeros_like(rhs_ref)
        n_ref[...] = jnp.zeros_like(n_ref)
        m_ref[...] = jnp.zeros_like(m_ref)

    src_i = lax.broadcasted_iota(jnp.int32, (t, t), 0)
    tgt_i = lax.broadcasted_iota(jnp.int32, (t, t), 1)
    first_head = lax.broadcasted_iota(jnp.int32, (t, 2 * dqk), 1) < dqk
    n_row = lax.broadcasted_iota(jnp.int32, (N_ROWS, 2 * t), 0)
    n_lane_first = lax.broadcasted_iota(jnp.int32, (N_ROWS, 2 * t), 1) < t
    own_block = ((n_row == 0) & n_lane_first) | ((n_row == 1) & jnp.logical_not(n_lane_first))
    n_row_q = lax.broadcasted_iota(jnp.int32, (N_ROWS, 2 * dqk), 0)
    streams = (
        (qf_ref, kf_ref, qtf_ref, ktf_ref, vf_ref, gcf_ref, grf_ref, hf_ref, src_i <= tgt_i, 0, t - 1),
        (qb_ref, kb_ref, qtb_ref, ktb_ref, vb_ref, gcb_ref, grb_ref, hb_ref, src_i >= tgt_i, 2 * heads, 0),
    )
    dn_nt = (((1,), (1,)), ((), ()))
    dn_tn = (((0,), (0,)), ((), ()))
    half = heads // 2

    def block_diag(a2):
        zero = jnp.zeros_like(a2)
        return jnp.concatenate([jnp.where(first_head, a2, zero), jnp.where(first_head, zero, a2)], axis=0)

    every = [(di, h, di * heads + h) for di in range(len(streams)) for h in range(heads)]
    pairs = [(di, p, di * half + p) for di in range(len(streams)) for p in range(half)]
    st = {}
    for di, h, idx in every:
        gc_ref, gr_ref, base, end = streams[di][5], streams[di][6], streams[di][9], streams[di][10]
        ig_r = gr_ref[0, base + h:base + h + 1, :]
        b_r = gr_ref[0, base + heads + h:base + heads + h + 1, :]
        total = b_r[:, end:end + 1]
        m_prev = m_ref[idx]
        st[idx] = dict(b_r=b_r, inter=b_r + m_prev, gk=total - b_r + ig_r, carry=total + m_prev,
                       src_c=gc_ref[0, :, base + h:base + h + 1] - gc_ref[0, :, base + heads + h:base + heads + h + 1])
    for di, h, idx in every:
        e = st[idx]
        m_new = jnp.maximum(e["carry"], jnp.max(e["gk"], axis=-1, keepdims=True))
        m_ref[idx] = m_new
        e["wk"] = jnp.exp(e["gk"] - m_new)
        e["decay"] = jnp.exp(e["carry"] - m_new)

    sp = {}
    for di, p, pi in pairs:
        sl = slice(2 * p * dqk, (2 * p + 2) * dqk)
        sp[pi] = dict(k2=streams[di][1][0, :, sl], n_prev=n_ref[pi])
        lhs = jnp.concatenate([sp[pi]["k2"], sp[pi]["n_prev"].astype(BF16)], axis=0)
        s2 = lax.dot_general(lhs, block_diag(streams[di][0][0, :, sl]), dn_nt, preferred_element_type=F32)
        for j in range(2):
            e = st[di * heads + 2 * p + j]
            e["qk"] = s2[:t, j * t:(j + 1) * t]
            e["qn"] = s2[t + j:t + j + 1, j * t:(j + 1) * t]

    for di, h, idx in every:
        e = st[idx]
        e["v"] = streams[di][4][0, :, h * dv:(h + 1) * dv]
        e["kwt"] = (streams[di][3][0, h * dqk:(h + 1) * dqk, :].astype(F32) * e["wk"]).astype(BF16)
    for di, h, idx in every:
        e = st[idx]
        e["upd"] = jnp.dot(e["kwt"], e["v"], preferred_element_type=F32)
        rhs_ref[par, idx, :t, :] = e["v"]
    for di, p, pi in pairs:
        ea, eb = st[di * heads + 2 * p], st[di * heads + 2 * p + 1]
        wk2 = jnp.concatenate([ea["wk"], eb["wk"]], axis=1)
        w_rows = jnp.where(own_block, wk2, 0.0).astype(BF16)
        dec = jnp.where(n_row_q == 0, ea["decay"], eb["decay"])
        n_ref[pi] = dec * sp[pi]["n_prev"] + jnp.dot(w_rows, block_diag(sp[pi]["k2"]), preferred_element_type=F32)
    for di, h, idx in every:
        e = st[idx]
        c_new = e["decay"] * c_ref[idx] + e["upd"]
        c_ref[idx] = c_new
        rhs_ref[1 - par, idx, t:, :] = c_new.astype(BF16)

    for di, h, idx in every:
        e = st[idx]
        e["dlog"] = jnp.where(streams[di][8], e["src_c"] + e["b_r"], -jnp.inf)
    for di, h, idx in every:
        e = st[idx]
        e["m_t"] = jnp.maximum(e["inter"], jnp.max(e["dlog"], axis=0, keepdims=True))
    for di, h, idx in every:
        e = st[idx]
        e["s"] = e["qk"] * jnp.exp(e["dlog"] - e["m_t"])
        e["sc"] = jnp.exp(e["inter"] - e["m_t"])
    for di, h, idx in every:
        e = st[idx]
        den = jnp.sum(e["s"], axis=0, keepdims=True) + e["sc"] * e["qn"]
        e["r"] = 1.0 / jnp.maximum(jnp.abs(den), jnp.exp(-e["m_t"]))
    for di, h, idx in every:
        e = st[idx]
        qt32 = streams[di][2][0, h * dqk:(h + 1) * dqk, :].astype(F32)
        e["lhs_t"] = jnp.concatenate(
            [(e["s"] * e["r"]).astype(BF16), (qt32 * (e["sc"] * e["r"])).astype(BF16)], axis=0)
    for di, h, idx in every:
        streams[di][7][0, :, h * dv:(h + 1) * dv] = lax.dot_general(
            st[idx]["lhs_t"], rhs_ref[par, idx], dn_tn, preferred_element_type=F32).astype(streams[di][7].dtype)


def _mlstm_scan(q, k, qt, kt, v, gc, gr, *, heads):
    b, l, mq = q.shape
    mv = v.shape[-1]
    dqk, dv = mq // heads, mv // heads
    t = min(SCAN_CHUNK, l)
    assert l % t == 0 and heads % 2 == 0
    nc = l // t
    ng = gr.shape[1]
    kern = functools.partial(_scan_kernel, t=t, heads=heads, dqk=dqk, dv=dv)
    fwd = lambda bi, c: (bi, c, 0)
    bwd = lambda bi, c: (bi, nc - 1 - c, 0)
    fwd_r = lambda bi, c: (bi, 0, c)
    bwd_r = lambda bi, c: (bi, 0, nc - 1 - c)

    def stream(tok, rowm):
        return [pl.BlockSpec((1, t, mq), tok), pl.BlockSpec((1, t, mq), tok),
                pl.BlockSpec((1, mq, t), rowm), pl.BlockSpec((1, mq, t), rowm), pl.BlockSpec((1, t, mv), tok),
                pl.BlockSpec((1, t, V7X_LANES), tok), pl.BlockSpec((1, ng, t), rowm)]

    return pl.pallas_call(
        kern,
        grid=(b, nc),
        in_specs=stream(fwd, fwd_r) + stream(bwd, bwd_r),
        out_specs=[pl.BlockSpec((1, t, mv), fwd), pl.BlockSpec((1, t, mv), bwd)],
        out_shape=[jax.ShapeDtypeStruct((b, l, mv), BF16), jax.ShapeDtypeStruct((b, l, mv), BF16)],
        scratch_shapes=[pltpu.VMEM((2 * heads, dqk, dv), F32), pltpu.VMEM((2, 2 * heads, t + dqk, dv), BF16),
                        pltpu.VMEM((heads, N_ROWS, 2 * dqk), F32), pltpu.VMEM((2 * heads, 1, 1), F32)],
        compiler_params=pltpu.CompilerParams(dimension_semantics=("parallel", "arbitrary")),
        name="mlstm_scan",
    )(q, k, qt, kt, v, gc, gr, q, k, qt, kt, v, gc, gr)


def _mout_kernel(hf_ref, hb_ref, o_ref, x_ref, hg_ref, w_ref, out_ref, y_ref, *, heads, dv):
    for h in range(heads):
        sl = slice(h * dv, (h + 1) * dv)
        hs = hf_ref[0, :, sl].astype(F32) + hb_ref[0, :, sl].astype(F32)
        hs = hs * lax.rsqrt(jnp.mean(hs * hs, axis=-1, keepdims=True) + EPS)
        hs = hs * hg_ref[:, sl]
        y_ref[:, sl] = (hs * jax.nn.sigmoid(o_ref[0, :, sl].astype(F32))).astype(BF16)
    out_ref[0] = x_ref[0] + jnp.dot(y_ref[...], w_ref[...], preferred_element_type=F32)


def _mlstm_out(hf, hb, o, x, head_g, w_out):
    b, l, d = x.shape
    heads, dv = head_g.shape
    mv = heads * dv
    tm = min(512, l)
    assert l % tm == 0
    row = lambda bi, i: (bi, i, 0)
    w = w_out.astype(BF16)
    vmem = w.size * 2 + 2 * tm * (3 * mv * 2 + 2 * d * 4) + tm * mv * 2 + 6 * tm * d * 4
    return pl.pallas_call(
        functools.partial(_mout_kernel, heads=heads, dv=dv),
        grid=(b, l // tm),
        in_specs=[pl.BlockSpec((1, tm, mv), row), pl.BlockSpec((1, tm, mv), row), pl.BlockSpec((1, tm, mv), row),
                  pl.BlockSpec((1, tm, d), row), _resident((1, mv)), _resident(w.shape)],
        out_specs=pl.BlockSpec((1, tm, d), row),
        out_shape=jax.ShapeDtypeStruct((b, l, d), F32),
        scratch_shapes=[pltpu.VMEM((tm, mv), BF16)],
        compiler_params=pltpu.CompilerParams(
            dimension_semantics=("parallel", "parallel"), vmem_limit_bytes=_vmem_limit(vmem)),
        name="mlstm_out",
    )(hf, hb, o, x, head_g.reshape(1, mv), w)


def _dft_tables(l1, l2, gd, n2h, n2b):
    l = l1 * l2
    a1 = 2.0 * np.pi * np.outer(np.arange(l1), np.arange(l1)) / l1
    wa = np.stack([np.cos(a1), -np.sin(a1)]) / np.sqrt(l1)
    wa = np.einsum("rkn,ij->krinj", wa, np.eye(n2h)).reshape(l1 * 2 * n2h, l1 * n2h)
    tw = 2.0 * np.pi * np.einsum("k,jhi->jhki", np.arange(l1),
                                 np.arange(l2).reshape(l2 // n2b, n2b // n2h, n2h)) / l
    twr, twi = np.cos(tw)[..., None], -np.sin(tw)[..., None]
    a2 = 2.0 * np.pi * np.outer(np.arange(l2), np.arange(l2)) / l2
    c2, s2 = np.cos(a2), np.sin(a2)
    wc = np.block([[c2, s2], [-s2, c2]]) / np.sqrt(l2)
    ac = 2.0 * np.pi * np.outer(np.arange(gd), np.arange(gd)) / gd
    cs = np.stack([np.cos(ac), np.sin(ac)]) / np.sqrt(gd)
    return (jnp.asarray(wa, BF16), jnp.asarray(twr, F32), jnp.asarray(twi, F32),
            jnp.asarray(wc, BF16), jnp.asarray(cs, BF16))


def _fa_kernel(x_ref, g_ref, wa_ref, twr_ref, twi_ref, out_ref, *, l1, n2h, n2b, d):
    g = g_ref[...]
    wa = wa_ref[...]
    halves = []
    for hh in range(n2b // n2h):
        x = x_ref[0, :, hh * n2h:(hh + 1) * n2h, :].reshape(l1 * n2h, d)
        a = jnp.dot(wa, _rmsnorm(x, g).astype(BF16), preferred_element_type=F32)
        a = a.reshape(l1, 2, n2h, d)
        ar = a[:, 0]
        ai = a[:, 1]
        tr = twr_ref[0, hh]
        ti = twi_ref[0, hh]
        halves.append(jnp.stack([ar * tr - ai * ti, ar * ti + ai * tr], axis=1))
    out_ref[0] = jnp.concatenate(halves, axis=2).astype(BF16)


def _fc_kernel(ba_ref, x_ref, wc_ref, cs_ref, wout_ref, bout_ref, out_ref, z_ref, f_ref, *, l2, k1b, d, gd):
    wc = wc_ref[...]
    for j in range(k1b):
        z = jnp.dot(wc, ba_ref[0, j], preferred_element_type=F32)
        z_ref[0, j * l2:(j + 1) * l2, :] = z[:l2].astype(BF16)
        z_ref[1, j * l2:(j + 1) * l2, :] = z[l2:].astype(BF16)
    for gi in range(d // gd):
        sl = slice(gi * gd, (gi + 1) * gd)
        f = (jnp.dot(z_ref[0, :, sl], cs_ref[0], preferred_element_type=F32)
             + jnp.dot(z_ref[1, :, sl], cs_ref[1], preferred_element_type=F32))
        f_ref[:, sl] = f.astype(BF16)
    y = jnp.dot(f_ref[...], wout_ref[...], preferred_element_type=F32) + bout_ref[...]
    for j in range(k1b):
        out_ref[0, :, j, :] = x_ref[0, :, j, :] + y[j * l2:(j + 1) * l2]


def _fourier_mixer(x, norm_g, w_out, b_out):
    b, l, d = x.shape
    l2 = min(DFT_INNER, l)
    l1 = l // l2
    assert l1 * l2 == l and d % F_GROUPS == 0
    gd = d // F_GROUPS
    n2h = V7X_SUBLANES
    n2b = min(V7X_BF16_ROWS, l2)
    k1b = min(V7X_SUBLANES, l1)
    assert l2 % n2b == 0 and n2b % n2h == 0 and l1 % k1b == 0
    wa, twr, twi, wc, cs = _dft_tables(l1, l2, gd, n2h, n2b)

    tw_spec = pl.BlockSpec((1, n2b // n2h, l1, n2h, 1), lambda bi, j: (j, 0, 0, 0, 0))
    ba = pl.pallas_call(
        functools.partial(_fa_kernel, l1=l1, n2h=n2h, n2b=n2b, d=d),
        grid=(b, l2 // n2b),
        in_specs=[pl.BlockSpec((1, l1, n2b, d), lambda bi, j: (bi, 0, j, 0)),
                  _resident((1, d)), _resident(wa.shape), tw_spec, tw_spec],
        out_specs=pl.BlockSpec((1, l1, 2, n2b, d), lambda bi, j: (bi, 0, 0, j, 0)),
        out_shape=jax.ShapeDtypeStruct((b, l1, 2, l2, d), BF16),
        compiler_params=pltpu.CompilerParams(
            dimension_semantics=("parallel", "parallel"),
            vmem_limit_bytes=_vmem_limit(l1 * n2b * d * (2 * 4 + 2 * 2 * 2 + 6 * 4) + wa.size * 2)),
        name="fourier_seq_a",
    )(x.reshape(b, l1, l2, d), norm_g.reshape(1, d), wa, twr, twi)

    rows = k1b * l2
    out = pl.pallas_call(
        functools.partial(_fc_kernel, l2=l2, k1b=k1b, d=d, gd=gd),
        grid=(b, l1 // k1b),
        in_specs=[pl.BlockSpec((1, k1b, 2 * l2, d), lambda bi, j: (bi, j, 0, 0)),
                  pl.BlockSpec((1, l2, k1b, d), lambda bi, j: (bi, 0, j, 0)),
                  _resident(wc.shape), _resident(cs.shape), _resident((d, d)), _resident((1, d))],
        out_specs=pl.BlockSpec((1, l2, k1b, d), lambda bi, j: (bi, 0, j, 0)),
        out_shape=jax.ShapeDtypeStruct((b, l2, l1, d), F32),
        scratch_shapes=[pltpu.VMEM((2, rows, d), BF16), pltpu.VMEM((rows, d), BF16)],
        compiler_params=pltpu.CompilerParams(
            dimension_semantics=("parallel", "parallel"),
            vmem_limit_bytes=_vmem_limit(rows * d * (2 * 2 * 2 + 4 * 4 + 3 * 2 + 4 * 4) + 4 * d * d)),
        name="fourier_seq_c",
    )(ba.reshape(b, l1, 2 * l2, d), x.reshape(b, l2, l1, d), wc, cs, w_out.astype(BF16), b_out.reshape(1, d))
    return out.reshape(b, l, d)


def _mlstm_mixer(x, norm_g, w_in, b_gate, head_g, w_out):
    heads, dv = head_g.shape
    q, k, qt, kt, v, o, gc, gr = _mlstm_proj(x, norm_g, w_in, b_gate, heads=heads, mv=heads * dv)
    hf, hb = _mlstm_scan(q, k, qt, kt, v, gc, gr, heads=heads)
    return _mlstm_out(hf, hb, o, x, head_g, w_out)


def _trunk(x, mix_norm_g, m_w_in, m_b_gate, m_head_g, m_w_out, f_w_out, f_b_out,
           ffn_norm_g, ffn_w_up, ffn_conv_w, ffn_conv_b, ffn_w_down, final_norm_g):
    depth = mix_norm_g.shape[0]
    n_mixers = 2
    for i in range(depth):
        j = i // n_mixers
        if i % n_mixers == 0:
            x = _mlstm_mixer(x, mix_norm_g[i], m_w_in[j], m_b_gate[j], m_head_g[j], m_w_out[j])
        else:
            x = _fourier_mixer(x, mix_norm_g[i], f_w_out[j], f_b_out[j])
        x = _conv_ffn(x, ffn_norm_g[i], ffn_w_up[i], ffn_conv_w[i], ffn_conv_b[i], ffn_w_down[i],
                      final_norm_g, final_norm=(i == depth - 1))
    return x


def kernel(x_prompt, x_sample, mix_norm_g, m_w_in, m_b_gate, m_head_g, m_w_out, f_w_out, f_b_out,
           ffn_norm_g, ffn_w_up, ffn_conv_w, ffn_conv_b, ffn_w_down, final_norm_g):
    params = (mix_norm_g, m_w_in, m_b_gate, m_head_g, m_w_out, f_w_out, f_b_out,
              ffn_norm_g, ffn_w_up, ffn_conv_w, ffn_conv_b, ffn_w_down, final_norm_g)
    return (_trunk(x_prompt, *params), _trunk(x_sample, *params))
```

```python
import functools

import numpy as np
import jax
import jax.numpy as jnp
from jax import lax
from jax.experimental import pallas as pl
from jax.experimental.pallas import tpu as pltpu

F32 = jnp.float32
BF16 = jnp.bfloat16

EPS = 1e-6
F_GROUPS = 4
N_GATE_KINDS = 4

V7X_VMEM_BYTES = 64 * 1024 * 1024
V7X_LANES = 128
V7X_SUBLANES = 8
V7X_BF16_ROWS = 16
V7X_MXU_DIM = 256

HALO = V7X_SUBLANES
SCAN_CHUNK = 128
N_ROWS = V7X_BF16_ROWS
DFT_INNER = 128


def _vmem_limit(nbytes):
    return int(min(max(nbytes, 32 * 1024 * 1024), V7X_VMEM_BYTES - 8 * 1024 * 1024))


def _rmsnorm(x, g):
    y = x * lax.rsqrt(jnp.mean(x * x, axis=-1, keepdims=True) + EPS)
    return y * g


def _resident(shape):
    nd = len(shape)
    return pl.BlockSpec(shape, lambda *_: (0,) * nd, pipeline_mode=pl.Buffered(1))


def _ffn_kernel(xm_ref, xp_ref, xn_ref, g_ref, wup_ref, cw_ref, cb_ref, wdn_ref, fg_ref,
                o_ref, h_ref, act_ref, *, tm, n_chunks, fc, conv_w, final_norm):
    i = pl.program_id(1)
    last = pl.num_programs(1) - 1
    g = g_ref[...]
    x = xm_ref[0]
    hp = jnp.where(i > 0, _rmsnorm(xp_ref[0], g), 0.0)
    hn = jnp.where(i < last, _rmsnorm(xn_ref[0], g), 0.0)
    h_ref[0:HALO, :] = hp.astype(BF16)
    h_ref[HALO:HALO + tm, :] = _rmsnorm(x, g).astype(BF16)
    h_ref[HALO + tm:, :] = hn.astype(BF16)
    hb = h_ref[...]
    pad = conv_w // 2
    d_ff = n_chunks * fc

    def conv_cols(lo):
        u = jnp.dot(hb, wup_ref[:, lo:lo + fc], preferred_element_type=F32)
        y = cb_ref[:, lo:lo + fc]
        for j in range(conv_w):
            off = HALO - pad + j
            y = y + u[off:off + tm] * cw_ref[j:j + 1, lo:lo + fc]
        return y

    for c in range(n_chunks):
        a = conv_cols(c * fc)
        val = conv_cols(d_ff + c * fc)
        act_ref[:, c * fc:(c + 1) * fc] = (a * jax.nn.sigmoid(a) * val).astype(BF16)
    r = x + jnp.dot(act_ref[...], wdn_ref[...], preferred_element_type=F32)
    if final_norm:
        r = _rmsnorm(r, fg_ref[...])
    o_ref[0] = r


def _conv_ffn(x, norm_g, w_up, conv_w, conv_b, w_down, final_g, *, final_norm):
    b, l, d = x.shape
    d_ff = w_down.shape[0]
    cw = conv_w.shape[0]
    fc = V7X_MXU_DIM
    n_chunks = d_ff // fc
    assert n_chunks * fc == d_ff and cw // 2 <= HALO
    tm = min(1024, l)
    assert l % tm == 0 and tm % HALO == 0
    nt = l // tm
    rb = tm // HALO
    assert (2 * d_ff) % V7X_LANES == 0 and fc % V7X_LANES == 0
    wup = w_up.astype(BF16)
    cwr = conv_w
    cbr = conv_b.reshape(1, 2 * d_ff)
    wdn = w_down.astype(BF16)
    kern = functools.partial(_ffn_kernel, tm=tm, n_chunks=n_chunks, fc=fc, conv_w=cw, final_norm=final_norm)
    vmem = (wup.size * 2 + wdn.size * 2 + 4 * tm * d * 4 + (tm + 2 * HALO) * d * 2 + tm * d_ff * 2
            + 8 * (tm + 2 * HALO) * fc * 4 + 3 * tm * d * 4)
    return pl.pallas_call(
        kern,
        grid=(b, nt),
        in_specs=[
            pl.BlockSpec((1, tm, d), lambda bi, i: (bi, i, 0)),
            pl.BlockSpec((1, HALO, d), lambda bi, i: (bi, jnp.maximum(i * rb - 1, 0), 0)),
            pl.BlockSpec((1, HALO, d), lambda bi, i: (bi, jnp.minimum((i + 1) * rb, l // HALO - 1), 0)),
            _resident((1, d)),
            _resident(wup.shape),
            _resident(cwr.shape),
            _resident(cbr.shape),
            _resident(wdn.shape),
            _resident((1, d)),
        ],
        out_specs=pl.BlockSpec((1, tm, d), lambda bi, i: (bi, i, 0)),
        out_shape=jax.ShapeDtypeStruct((b, l, d), F32),
        scratch_shapes=[pltpu.VMEM((tm + 2 * HALO, d), BF16), pltpu.VMEM((tm, d_ff), BF16)],
        compiler_params=pltpu.CompilerParams(
            dimension_semantics=("parallel", "arbitrary"), vmem_limit_bytes=_vmem_limit(vmem)),
        name="conv_ffn",
    )(x, x, x, norm_g.reshape(1, d), wup, cwr, cbr, wdn, final_g.reshape(1, d))


def _split3(x):
    hi = x.astype(BF16)
    r = x - hi.astype(F32)
    mid = r.astype(BF16)
    lo = (r - mid.astype(F32)).astype(BF16)
    return hi, mid, lo


def _proj_kernel(x_ref, g_ref, w_ref, wg_ref, bg_ref, pre_ref, suf_ref,
                 q_ref, k_ref, qt_ref, kt_ref, v_ref, gc_ref, gr_ref, *, mq, mv, heads, k_scale):
    h = _rmsnorm(x_ref[0], g_ref[...]).astype(BF16)
    q = jnp.dot(h, w_ref[:, 0:mq], preferred_element_type=F32)
    k = jnp.dot(h, w_ref[:, mq:2 * mq], preferred_element_type=F32) * k_scale
    q_ref[0] = q.astype(BF16)
    k_ref[0] = k.astype(BF16)
    qt_ref[0] = q.T.astype(BF16)
    kt_ref[0] = k.T.astype(BF16)
    v_ref[0] = jnp.dot(h, w_ref[:, 2 * mq:2 * mq + mv], preferred_element_type=F32).astype(BF16)
    gt = jnp.dot(h, wg_ref[...], preferred_element_type=F32) + bg_ref[...]
    lane = lax.broadcasted_iota(jnp.int32, gt.shape, 1)
    kind = lane // heads
    log_sig = jnp.minimum(gt, 0.0) - jnp.log1p(jnp.exp(-jnp.abs(gt)))
    parts = _split3(log_sig)
    t = pre_ref.shape[0]
    pre, suf = [], []
    for c in range(gt.shape[0] // t):
        rows = slice(c * t, (c + 1) * t)
        pre.append(sum(jnp.dot(pre_ref[...], p[rows], preferred_element_type=F32) for p in parts))
        suf.append(sum(jnp.dot(suf_ref[...], p[rows], preferred_element_type=F32) for p in parts))
    gt = jnp.where(kind == 1, jnp.concatenate(pre, axis=0), jnp.where(kind == 3, jnp.concatenate(suf, axis=0), gt))
    gc_ref[0] = gt
    gr_ref[0] = gt.T[0:gr_ref.shape[1], :]


def _mlstm_proj(x, norm_g, w_in, b_gate, *, heads, mv):
    b, l, d = x.shape
    ng = N_GATE_KINDS * heads
    mq = (w_in.shape[1] - 2 * mv - ng) // 2
    dqk = mq // heads
    tm = min(1024, l)
    assert l % tm == 0 and ng <= V7X_LANES
    w_main = w_in[:, :2 * mq + mv].astype(BF16)
    w_gate = jnp.pad(w_in[:, 2 * mq + 2 * mv:], ((0, 0), (0, V7X_LANES - ng))).astype(BF16)
    b_pad = jnp.pad(b_gate, (0, V7X_LANES - ng)).reshape(1, V7X_LANES)
    t = min(SCAN_CHUNK, l)
    assert tm % t == 0
    pos = np.arange(t)
    pre_m = jnp.asarray(pos[None, :] <= pos[:, None], BF16)
    suf_m = jnp.asarray(pos[None, :] >= pos[:, None], BF16)
    kern = functools.partial(_proj_kernel, mq=mq, mv=mv, heads=heads, k_scale=float(dqk) ** -0.5)
    row = lambda bi, i: (bi, i, 0)
    col = lambda bi, i: (bi, 0, i)
    vmem = (w_main.size * 2 + w_gate.size * 2 + 2 * tm * d * 4 + 2 * tm * (4 * mq + mv) * 2 + 8 * tm * mv * 4)
    return pl.pallas_call(
        kern,
        grid=(b, l // tm),
        in_specs=[
            pl.BlockSpec((1, tm, d), row),
            _resident((1, d)),
            _resident(w_main.shape),
            _resident(w_gate.shape),
            _resident((1, V7X_LANES)),
            _resident((t, t)),
            _resident((t, t)),
        ],
        out_specs=[
            pl.BlockSpec((1, tm, mq), row),
            pl.BlockSpec((1, tm, mq), row),
            pl.BlockSpec((1, mq, tm), col),
            pl.BlockSpec((1, mq, tm), col),
            pl.BlockSpec((1, tm, mv), row),
            pl.BlockSpec((1, tm, V7X_LANES), row),
            pl.BlockSpec((1, ng, tm), col),
        ],
        out_shape=[
            jax.ShapeDtypeStruct((b, l, mq), BF16),
            jax.ShapeDtypeStruct((b, l, mq), BF16),
            jax.ShapeDtypeStruct((b, mq, l), BF16),
            jax.ShapeDtypeStruct((b, mq, l), BF16),
            jax.ShapeDtypeStruct((b, l, mv), BF16),
            jax.ShapeDtypeStruct((b, l, V7X_LANES), F32),
            jax.ShapeDtypeStruct((b, ng, l), F32),
        ],
        compiler_params=pltpu.CompilerParams(
            dimension_semantics=("parallel", "parallel"), vmem_limit_bytes=_vmem_limit(vmem)),
        name="mlstm_proj",
    )(x, norm_g.reshape(1, d), w_main, w_gate, b_pad, pre_m, suf_m)


def _scan_kernel(qf_ref, kf_ref, qtf_ref, ktf_ref, vf_ref, gcf_ref, grf_ref,
                 qb_ref, kb_ref, qtb_ref, ktb_ref, vb_ref, gcb_ref, grb_ref,
                 hf_ref, hb_ref, c_ref, rhs_ref, n_ref, m_ref, *, t, heads, dqk, dv):
    step = pl.program_id(1)
    par = step % 2

    @pl.when(step == 0)
    def _():
        c_ref[...] = jnp.zeros_like(c_ref)
        rhs_ref[...] = jnp.zeros_like(rhs_ref)
        n_ref[...] = jnp.zeros_like(n_ref)
        m_ref[...] = jnp.zeros_like(m_ref)

    src_i = lax.broadcasted_iota(jnp.int32, (t, t), 0)
    tgt_i = lax.broadcasted_iota(jnp.int32, (t, t), 1)
    first_head = lax.broadcasted_iota(jnp.int32, (t, 2 * dqk), 1) < dqk
    n_row = lax.broadcasted_iota(jnp.int32, (N_ROWS, 2 * t), 0)
    n_lane_first = lax.broadcasted_iota(jnp.int32, (N_ROWS, 2 * t), 1) < t
    own_block = ((n_row == 0) & n_lane_first) | ((n_row == 1) & jnp.logical_not(n_lane_first))
    n_row_q = lax.broadcasted_iota(jnp.int32, (N_ROWS, 2 * dqk), 0)
    streams = (
        (qf_ref, kf_ref, qtf_ref, ktf_ref, vf_ref, gcf_ref, grf_ref, hf_ref, src_i <= tgt_i, 0, t - 1),
        (qb_ref, kb_ref, qtb_ref, ktb_ref, vb_ref, gcb_ref, grb_ref, hb_ref, src_i >= tgt_i, 2 * heads, 0),
    )
    dn_nt = (((1,), (1,)), ((), ()))
    dn_tn = (((0,), (0,)), ((), ()))
    half = heads // 2

    def block_diag(a2):
        zero = jnp.zeros_like(a2)
        return jnp.concatenate([jnp.where(first_head, a2, zero), jnp.where(first_head, zero, a2)], axis=0)

    every = [(di, h, di * heads + h) for di in range(len(streams)) for h in range(heads)]
    pairs = [(di, p, di * half + p) for di in range(len(streams)) for p in range(half)]
    st = {}
    for di, h, idx in every:
        gc_ref, gr_ref, base, end = streams[di][5], streams[di][6], streams[di][9], streams[di][10]
        ig_r = gr_ref[0, base + h:base + h + 1, :]
        b_r = gr_ref[0, base + heads + h:base + heads + h + 1, :]
        total = b_r[:, end:end + 1]
        m_prev = m_ref[idx]
        st[idx] = dict(b_r=b_r, inter=b_r + m_prev, gk=total - b_r + ig_r, carry=total + m_prev,
                       src_c=gc_ref[0, :, base + h:base + h + 1] - gc_ref[0, :, base + heads + h:base + heads + h + 1])
    for di, h, idx in every:
        e = st[idx]
        m_new = jnp.maximum(e["carry"], jnp.max(e["gk"], axis=-1, keepdims=True))
        m_ref[idx] = m_new
        e["wk"] = jnp.exp(e["gk"] - m_new)
        e["decay"] = jnp.exp(e["carry"] - m_new)

    sp = {}
    for di, p, pi in pairs:
        sl = slice(2 * p * dqk, (2 * p + 2) * dqk)
        sp[pi] = dict(k2=streams[di][1][0, :, sl], n_prev=n_ref[pi])
        lhs = jnp.concatenate([sp[pi]["k2"], sp[pi]["n_prev"].astype(BF16)], axis=0)
        s2 = lax.dot_general(lhs, block_diag(streams[di][0][0, :, sl]), dn_nt, preferred_element_type=F32)
        for j in range(2):
            e = st[di * heads + 2 * p + j]
            e["qk"] = s2[:t, j * t:(j + 1) * t]
            e["qn"] = s2[t + j:t + j + 1, j * t:(j + 1) * t]

    for di, h, idx in every:
        e = st[idx]
        e["v"] = streams[di][4][0, :, h * dv:(h + 1) * dv]
        e["kwt"] = (streams[di][3][0, h * dqk:(h + 1) * dqk, :].astype(F32) * e["wk"]).astype(BF16)
    for di, h, idx in every:
        e = st[idx]
        e["upd"] = jnp.dot(e["kwt"], e["v"], preferred_element_type=F32)
        rhs_ref[par, idx, :t, :] = e["v"]
    for di, p, pi in pairs:
        ea, eb = st[di * heads + 2 * p], st[di * heads + 2 * p + 1]
        wk2 = jnp.concatenate([ea["wk"], eb["wk"]], axis=1)
        w_rows = jnp.where(own_block, wk2, 0.0).astype(BF16)
        dec = jnp.where(n_row_q == 0, ea["decay"], eb["decay"])
        n_ref[pi] = dec * sp[pi]["n_prev"] + jnp.dot(w_rows, block_diag(sp[pi]["k2"]), preferred_element_type=F32)
    for di, h, idx in every:
        e = st[idx]
        c_new = e["decay"] * c_ref[idx] + e["upd"]
        c_ref[idx] = c_new
        rhs_ref[1 - par, idx, t:, :] = c_new.astype(BF16)

    for di, h, idx in every:
        e = st[idx]
        e["dlog"] = jnp.where(streams[di][8], e["src_c"] + e["b_r"], -jnp.inf)
    for di, h, idx in every:
        e = st[idx]
        e["m_t"] = jnp.maximum(e["inter"], jnp.max(e["dlog"], axis=0, keepdims=True))
    for di, h, idx in every:
        e = st[idx]
        e["s"] = e["qk"] * jnp.exp(e["dlog"] - e["m_t"])
        e["sc"] = jnp.exp(e["inter"] - e["m_t"])
    for di, h, idx in every:
        e = st[idx]
        den = jnp.sum(e["s"], axis=0, keepdims=True) + e["sc"] * e["qn"]
        e["r"] = 1.0 / jnp.maximum(jnp.abs(den), jnp.exp(-e["m_t"]))
    for di, h, idx in every:
        e = st[idx]
        qt32 = streams[di][2][0, h * dqk:(h + 1) * dqk, :].astype(F32)
        e["lhs_t"] = jnp.concatenate(
            [(e["s"] * e["r"]).astype(BF16), (qt32 * (e["sc"] * e["r"])).astype(BF16)], axis=0)
    for di, h, idx in every:
        streams[di][7][0, :, h * dv:(h + 1) * dv] = lax.dot_general(
            st[idx]["lhs_t"], rhs_ref[par, idx], dn_tn, preferred_element_type=F32).astype(streams[di][7].dtype)


def _mlstm_scan(q, k, qt, kt, v, gc, gr, *, heads):
    b, l, mq = q.shape
    mv = v.shape[-1]
    dqk, dv = mq // heads, mv // heads
    t = min(SCAN_CHUNK, l)
    assert l % t == 0 and heads % 2 == 0
    nc = l // t
    ng = gr.shape[1]
    kern = functools.partial(_scan_kernel, t=t, heads=heads, dqk=dqk, dv=dv)
    fwd = lambda bi, c: (bi, c, 0)
    bwd = lambda bi, c: (bi, nc - 1 - c, 0)
    fwd_r = lambda bi, c: (bi, 0, c)
    bwd_r = lambda bi, c: (bi, 0, nc - 1 - c)

    def stream(tok, rowm):
        return [pl.BlockSpec((1, t, mq), tok), pl.BlockSpec((1, t, mq), tok),
                pl.BlockSpec((1, mq, t), rowm), pl.BlockSpec((1, mq, t), rowm), pl.BlockSpec((1, t, mv), tok),
                pl.BlockSpec((1, t, V7X_LANES), tok), pl.BlockSpec((1, ng, t), rowm)]

    return pl.pallas_call(
        kern,
        grid=(b, nc),
        in_specs=stream(fwd, fwd_r) + stream(bwd, bwd_r),
        out_specs=[pl.BlockSpec((1, t, mv), fwd), pl.BlockSpec((1, t, mv), bwd)],
        out_shape=[jax.ShapeDtypeStruct((b, l, mv), BF16), jax.ShapeDtypeStruct((b, l, mv), BF16)],
        scratch_shapes=[pltpu.VMEM((2 * heads, dqk, dv), F32), pltpu.VMEM((2, 2 * heads, t + dqk, dv), BF16),
                        pltpu.VMEM((heads, N_ROWS, 2 * dqk), F32), pltpu.VMEM((2 * heads, 1, 1), F32)],
        compiler_params=pltpu.CompilerParams(dimension_semantics=("parallel", "arbitrary")),
        name="mlstm_scan",
    )(q, k, qt, kt, v, gc, gr, q, k, qt, kt, v, gc, gr)


def _mout_kernel(hf_ref, hb_ref, x_ref, g_ref, wo_ref, hg_ref, w_ref, out_ref, y_ref, *, heads, dv):
    x = x_ref[0]
    o = jnp.dot(_rmsnorm(x, g_ref[...]).astype(BF16), wo_ref[...], preferred_element_type=F32)
    for h in range(heads):
        sl = slice(h * dv, (h + 1) * dv)
        hs = hf_ref[0, :, sl].astype(F32) + hb_ref[0, :, sl].astype(F32)
        hs = hs * lax.rsqrt(jnp.mean(hs * hs, axis=-1, keepdims=True) + EPS)
        hs = hs * hg_ref[:, sl]
        y_ref[:, sl] = (hs * jax.nn.sigmoid(o[:, sl])).astype(BF16)
    out_ref[0] = x + jnp.dot(y_ref[...], w_ref[...], preferred_element_type=F32)


def _mlstm_out(hf, hb, x, norm_g, w_gate_out, head_g, w_out):
    b, l, d = x.shape
    heads, dv = head_g.shape
    mv = heads * dv
    tm = min(1024, l)
    assert l % tm == 0
    row = lambda bi, i: (bi, i, 0)
    w = w_out.astype(BF16)
    wo = w_gate_out.astype(BF16)
    vmem = (w.size + wo.size) * 2 + 2 * tm * (2 * mv * 2 + 2 * d * 4) + tm * mv * 2 + 8 * tm * d * 4
    return pl.pallas_call(
        functools.partial(_mout_kernel, heads=heads, dv=dv),
        grid=(b, l // tm),
        in_specs=[pl.BlockSpec((1, tm, mv), row), pl.BlockSpec((1, tm, mv), row), pl.BlockSpec((1, tm, d), row),
                  _resident((1, d)), _resident(wo.shape), _resident((1, mv)), _resident(w.shape)],
        out_specs=pl.BlockSpec((1, tm, d), row),
        out_shape=jax.ShapeDtypeStruct((b, l, d), F32),
        scratch_shapes=[pltpu.VMEM((tm, mv), BF16)],
        compiler_params=pltpu.CompilerParams(
            dimension_semantics=("parallel", "parallel"), vmem_limit_bytes=_vmem_limit(vmem)),
        name="mlstm_out",
    )(hf, hb, x, norm_g.reshape(1, d), wo, head_g.reshape(1, mv), w)


def _dft_tables(l1, l2, gd, n2h, n2b):
    l = l1 * l2
    a1 = 2.0 * np.pi * np.outer(np.arange(l1), np.arange(l1)) / l1
    wa = np.stack([np.cos(a1), -np.sin(a1)]) / np.sqrt(l1)
    wa = np.einsum("rkn,ij->krinj", wa, np.eye(n2h)).reshape(l1 * 2 * n2h, l1 * n2h)
    tw = 2.0 * np.pi * np.einsum("k,jhi->jhki", np.arange(l1),
                                 np.arange(l2).reshape(l2 // n2b, n2b // n2h, n2h)) / l
    twr, twi = np.cos(tw)[..., None], -np.sin(tw)[..., None]
    a2 = 2.0 * np.pi * np.outer(np.arange(l2), np.arange(l2)) / l2
    c2, s2 = np.cos(a2), np.sin(a2)
    wc = np.block([[c2, s2], [-s2, c2]]) / np.sqrt(l2)
    ac = 2.0 * np.pi * np.outer(np.arange(gd), np.arange(gd)) / gd
    cs = np.stack([np.cos(ac), np.sin(ac)]) / np.sqrt(gd)
    return (jnp.asarray(wa, BF16), jnp.asarray(twr, F32), jnp.asarray(twi, F32),
            jnp.asarray(wc, BF16), jnp.asarray(cs, BF16))


def _fa_kernel(x_ref, g_ref, wa_ref, twr_ref, twi_ref, out_ref, *, l1, n2h, n2b, d):
    g = g_ref[...]
    wa = wa_ref[...]
    halves = []
    for hh in range(n2b // n2h):
        x = x_ref[0, :, hh * n2h:(hh + 1) * n2h, :].reshape(l1 * n2h, d)
        a = jnp.dot(wa, _rmsnorm(x, g).astype(BF16), preferred_element_type=F32)
        a = a.reshape(l1, 2, n2h, d)
        ar = a[:, 0]
        ai = a[:, 1]
        tr = twr_ref[0, hh]
        ti = twi_ref[0, hh]
        halves.append(jnp.stack([ar * tr - ai * ti, ar * ti + ai * tr], axis=1))
    out_ref[0] = jnp.concatenate(halves, axis=2).astype(BF16)


def _fc_kernel(ba_ref, x_ref, wc_ref, cs_ref, wout_ref, bout_ref, out_ref, z_ref, f_ref, *, l2, k1b, d, gd):
    wc = wc_ref[...]
    for j in range(k1b):
        z = jnp.dot(wc, ba_ref[0, j], preferred_element_type=F32)
        z_ref[0, j * l2:(j + 1) * l2, :] = z[:l2].astype(BF16)
        z_ref[1, j * l2:(j + 1) * l2, :] = z[l2:].astype(BF16)
    for gi in range(d // gd):
        sl = slice(gi * gd, (gi + 1) * gd)
        f = (jnp.dot(z_ref[0, :, sl], cs_ref[0], preferred_element_type=F32)
             + jnp.dot(z_ref[1, :, sl], cs_ref[1], preferred_element_type=F32))
        f_ref[:, sl] = f.astype(BF16)
    y = jnp.dot(f_ref[...], wout_ref[...], preferred_element_type=F32) + bout_ref[...]
    for j in range(k1b):
        out_ref[0, :, j, :] = x_ref[0, :, j, :] + y[j * l2:(j + 1) * l2]


def _fourier_mixer(x, norm_g, w_out, b_out):
    b, l, d = x.shape
    l2 = min(DFT_INNER, l)
    l1 = l // l2
    assert l1 * l2 == l and d % F_GROUPS == 0
    gd = d // F_GROUPS
    n2h = V7X_SUBLANES
    n2b = min(V7X_BF16_ROWS, l2)
    k1b = min(V7X_SUBLANES, l1)
    assert l2 % n2b == 0 and n2b % n2h == 0 and l1 % k1b == 0
    wa, twr, twi, wc, cs = _dft_tables(l1, l2, gd, n2h, n2b)

    tw_spec = pl.BlockSpec((1, n2b // n2h, l1, n2h, 1), lambda bi, j: (j, 0, 0, 0, 0))
    ba = pl.pallas_call(
        functools.partial(_fa_kernel, l1=l1, n2h=n2h, n2b=n2b, d=d),
        grid=(b, l2 // n2b),
        in_specs=[pl.BlockSpec((1, l1, n2b, d), lambda bi, j: (bi, 0, j, 0)),
                  _resident((1, d)), _resident(wa.shape), tw_spec, tw_spec],
        out_specs=pl.BlockSpec((1, l1, 2, n2b, d), lambda bi, j: (bi, 0, 0, j, 0)),
        out_shape=jax.ShapeDtypeStruct((b, l1, 2, l2, d), BF16),
        compiler_params=pltpu.CompilerParams(
            dimension_semantics=("parallel", "parallel"),
            vmem_limit_bytes=_vmem_limit(l1 * n2b * d * (2 * 4 + 2 * 2 * 2 + 6 * 4) + wa.size * 2)),
        name="fourier_seq_a",
    )(x.reshape(b, l1, l2, d), norm_g.reshape(1, d), wa, twr, twi)

    rows = k1b * l2
    out = pl.pallas_call(
        functools.partial(_fc_kernel, l2=l2, k1b=k1b, d=d, gd=gd),
        grid=(b, l1 // k1b),
        in_specs=[pl.BlockSpec((1, k1b, 2 * l2, d), lambda bi, j: (bi, j, 0, 0)),
                  pl.BlockSpec((1, l2, k1b, d), lambda bi, j: (bi, 0, j, 0)),
                  _resident(wc.shape), _resident(cs.shape), _resident((d, d)), _resident((1, d))],
        out_specs=pl.BlockSpec((1, l2, k1b, d), lambda bi, j: (bi, 0, j, 0)),
        out_shape=jax.ShapeDtypeStruct((b, l2, l1, d), F32),
        scratch_shapes=[pltpu.VMEM((2, rows, d), BF16), pltpu.VMEM((rows, d), BF16)],
        compiler_params=pltpu.CompilerParams(
            dimension_semantics=("parallel", "parallel"),
            vmem_limit_bytes=_vmem_limit(rows * d * (2 * 2 * 2 + 4 * 4 + 3 * 2 + 4 * 4) + 4 * d * d)),
        name="fourier_seq_c",
    )(ba.reshape(b, l1, 2 * l2, d), x.reshape(b, l2, l1, d), wc, cs, w_out.astype(BF16), b_out.reshape(1, d))
    return out.reshape(b, l, d)


def _mlstm_mixer(x, norm_g, w_in, b_gate, head_g, w_out):
    heads, dv = head_g.shape
    mv = heads * dv
    q, k, qt, kt, v, gc, gr = _mlstm_proj(x, norm_g, w_in, b_gate, heads=heads, mv=mv)
    hf, hb = _mlstm_scan(q, k, qt, kt, v, gc, gr, heads=heads)
    o_lo = 2 * q.shape[-1] + mv
    return _mlstm_out(hf, hb, x, norm_g, w_in[:, o_lo:o_lo + mv], head_g, w_out)


def _trunk(x, mix_norm_g, m_w_in, m_b_gate, m_head_g, m_w_out, f_w_out, f_b_out,
           ffn_norm_g, ffn_w_up, ffn_conv_w, ffn_conv_b, ffn_w_down, final_norm_g):
    depth = mix_norm_g.shape[0]
    n_mixers = 2
    for i in range(depth):
        j = i // n_mixers
        if i % n_mixers == 0:
            x = _mlstm_mixer(x, mix_norm_g[i], m_w_in[j], m_b_gate[j], m_head_g[j], m_w_out[j])
        else:
            x = _fourier_mixer(x, mix_norm_g[i], f_w_out[j], f_b_out[j])
        x = _conv_ffn(x, ffn_norm_g[i], ffn_w_up[i], ffn_conv_w[i], ffn_conv_b[i], ffn_w_down[i],
                      final_norm_g, final_norm=(i == depth - 1))
    return x


def kernel(x_prompt, x_sample, mix_norm_g, m_w_in, m_b_gate, m_head_g, m_w_out, f_w_out, f_b_out,
           ffn_norm_g, ffn_w_up, ffn_conv_w, ffn_conv_b, ffn_w_down, final_norm_g):
    params = (mix_norm_g, m_w_in, m_b_gate, m_head_g, m_w_out, f_w_out, f_b_out,
              ffn_norm_g, ffn_w_up, ffn_conv_w, ffn_conv_b, ffn_w_down, final_norm_g)
    return (_trunk(x_prompt, *params), _trunk(x_sample, *params))
```

```python
import functools

import numpy as np
import jax
import jax.numpy as jnp
from jax import lax
from jax.experimental import pallas as pl
from jax.experimental.pallas import tpu as pltpu

F32 = jnp.float32
BF16 = jnp.bfloat16

EPS = 1e-6
F_GROUPS = 4
N_GATE_KINDS = 4

V7X_VMEM_BYTES = 64 * 1024 * 1024
V7X_LANES = 128
V7X_SUBLANES = 8
V7X_BF16_ROWS = 16
V7X_MXU_DIM = 256

HALO = V7X_SUBLANES
SCAN_CHUNK = 128
N_ROWS = V7X_BF16_ROWS
SCAN_ROWS = 2
DFT_INNER = 128


def _vmem_limit(nbytes):
    return int(min(max(nbytes, 32 * 1024 * 1024), V7X_VMEM_BYTES - 8 * 1024 * 1024))


def _rmsnorm(x, g):
    y = x * lax.rsqrt(jnp.mean(x * x, axis=-1, keepdims=True) + EPS)
    return y * g


def _resident(shape):
    nd = len(shape)
    return pl.BlockSpec(shape, lambda *_: (0,) * nd, pipeline_mode=pl.Buffered(1))


def _ffn_kernel(xm_ref, xp_ref, xn_ref, g_ref, wup_ref, cw_ref, cb_ref, wdn_ref, fg_ref,
                o_ref, h_ref, act_ref, *, tm, n_chunks, fc, conv_w, final_norm):
    i = pl.program_id(1)
    last = pl.num_programs(1) - 1
    g = g_ref[...]
    x = xm_ref[0]
    hp = jnp.where(i > 0, _rmsnorm(xp_ref[0], g), 0.0)
    hn = jnp.where(i < last, _rmsnorm(xn_ref[0], g), 0.0)
    h_ref[0:HALO, :] = hp.astype(BF16)
    h_ref[HALO:HALO + tm, :] = _rmsnorm(x, g).astype(BF16)
    h_ref[HALO + tm:, :] = hn.astype(BF16)
    hb = h_ref[...]
    pad = conv_w // 2
    d_ff = n_chunks * fc

    def conv_cols(lo):
        u = jnp.dot(hb, wup_ref[:, lo:lo + fc], preferred_element_type=F32)
        y = cb_ref[:, lo:lo + fc]
        for j in range(conv_w):
            off = HALO - pad + j
            y = y + u[off:off + tm] * cw_ref[j:j + 1, lo:lo + fc]
        return y

    for c in range(n_chunks):
        a = conv_cols(c * fc)
        val = conv_cols(d_ff + c * fc)
        act_ref[:, c * fc:(c + 1) * fc] = (a * jax.nn.sigmoid(a) * val).astype(BF16)
    r = x + jnp.dot(act_ref[...], wdn_ref[...], preferred_element_type=F32)
    if final_norm:
        r = _rmsnorm(r, fg_ref[...])
    o_ref[0] = r


def _conv_ffn(x, norm_g, w_up, conv_w, conv_b, w_down, final_g, *, final_norm):
    b, l, d = x.shape
    d_ff = w_down.shape[0]
    cw = conv_w.shape[0]
    fc = V7X_MXU_DIM
    n_chunks = d_ff // fc
    assert n_chunks * fc == d_ff and cw // 2 <= HALO
    tm = min(1024, l)
    assert l % tm == 0 and tm % HALO == 0
    nt = l // tm
    rb = tm // HALO
    assert (2 * d_ff) % V7X_LANES == 0 and fc % V7X_LANES == 0
    wup = w_up.astype(BF16)
    cwr = conv_w
    cbr = conv_b.reshape(1, 2 * d_ff)
    wdn = w_down.astype(BF16)
    kern = functools.partial(_ffn_kernel, tm=tm, n_chunks=n_chunks, fc=fc, conv_w=cw, final_norm=final_norm)
    vmem = (wup.size * 2 + wdn.size * 2 + 4 * tm * d * 4 + (tm + 2 * HALO) * d * 2 + tm * d_ff * 2
            + 8 * (tm + 2 * HALO) * fc * 4 + 3 * tm * d * 4)
    return pl.pallas_call(
        kern,
        grid=(b, nt),
        in_specs=[
            pl.BlockSpec((1, tm, d), lambda bi, i: (bi, i, 0)),
            pl.BlockSpec((1, HALO, d), lambda bi, i: (bi, jnp.maximum(i * rb - 1, 0), 0)),
            pl.BlockSpec((1, HALO, d), lambda bi, i: (bi, jnp.minimum((i + 1) * rb, l // HALO - 1), 0)),
            _resident((1, d)),
            _resident(wup.shape),
            _resident(cwr.shape),
            _resident(cbr.shape),
            _resident(wdn.shape),
            _resident((1, d)),
        ],
        out_specs=pl.BlockSpec((1, tm, d), lambda bi, i: (bi, i, 0)),
        out_shape=jax.ShapeDtypeStruct((b, l, d), F32),
        scratch_shapes=[pltpu.VMEM((tm + 2 * HALO, d), BF16), pltpu.VMEM((tm, d_ff), BF16)],
        compiler_params=pltpu.CompilerParams(
            dimension_semantics=("parallel", "arbitrary"), vmem_limit_bytes=_vmem_limit(vmem)),
        name="conv_ffn",
    )(x, x, x, norm_g.reshape(1, d), wup, cwr, cbr, wdn, final_g.reshape(1, d))


def _split3(x):
    hi = x.astype(BF16)
    r = x - hi.astype(F32)
    mid = r.astype(BF16)
    lo = (r - mid.astype(F32)).astype(BF16)
    return hi, mid, lo


def _proj_kernel(x_ref, g_ref, w_ref, wg_ref, bg_ref, pre_ref, suf_ref,
                 q_ref, k_ref, qt_ref, kt_ref, v_ref, gc_ref, gr_ref, *, mq, mv, heads, k_scale):
    h = _rmsnorm(x_ref[0], g_ref[...]).astype(BF16)
    q = jnp.dot(h, w_ref[:, 0:mq], preferred_element_type=F32)
    k = jnp.dot(h, w_ref[:, mq:2 * mq], preferred_element_type=F32) * k_scale
    q_ref[0] = q.astype(BF16)
    k_ref[0] = k.astype(BF16)
    qt_ref[0] = q.T.astype(BF16)
    kt_ref[0] = k.T.astype(BF16)
    v_ref[0] = jnp.dot(h, w_ref[:, 2 * mq:2 * mq + mv], preferred_element_type=F32).astype(BF16)
    gt = jnp.dot(h, wg_ref[...], preferred_element_type=F32) + bg_ref[...]
    lane = lax.broadcasted_iota(jnp.int32, gt.shape, 1)
    kind = lane // heads
    log_sig = jnp.minimum(gt, 0.0) - jnp.log1p(jnp.exp(-jnp.abs(gt)))
    parts = _split3(log_sig)
    t = pre_ref.shape[0]
    pre, suf = [], []
    for c in range(gt.shape[0] // t):
        rows = slice(c * t, (c + 1) * t)
        pre.append(sum(jnp.dot(pre_ref[...], p[rows], preferred_element_type=F32) for p in parts))
        suf.append(sum(jnp.dot(suf_ref[...], p[rows], preferred_element_type=F32) for p in parts))
    gt = jnp.where(kind == 1, jnp.concatenate(pre, axis=0), jnp.where(kind == 3, jnp.concatenate(suf, axis=0), gt))
    gc_ref[0] = gt
    gr_ref[0] = gt.T[0:gr_ref.shape[1], :]


def _mlstm_proj(x, norm_g, w_in, b_gate, *, heads, mv):
    b, l, d = x.shape
    ng = N_GATE_KINDS * heads
    mq = (w_in.shape[1] - 2 * mv - ng) // 2
    dqk = mq // heads
    tm = min(1024, l)
    assert l % tm == 0 and ng <= V7X_LANES
    w_main = w_in[:, :2 * mq + mv].astype(BF16)
    w_gate = jnp.pad(w_in[:, 2 * mq + 2 * mv:], ((0, 0), (0, V7X_LANES - ng))).astype(BF16)
    b_pad = jnp.pad(b_gate, (0, V7X_LANES - ng)).reshape(1, V7X_LANES)
    t = min(SCAN_CHUNK, l)
    assert tm % t == 0
    pos = np.arange(t)
    pre_m = jnp.asarray(pos[None, :] <= pos[:, None], BF16)
    suf_m = jnp.asarray(pos[None, :] >= pos[:, None], BF16)
    kern = functools.partial(_proj_kernel, mq=mq, mv=mv, heads=heads, k_scale=float(dqk) ** -0.5)
    row = lambda bi, i: (bi, i, 0)
    col = lambda bi, i: (bi, 0, i)
    vmem = (w_main.size * 2 + w_gate.size * 2 + 2 * tm * d * 4 + 2 * tm * (4 * mq + mv) * 2 + 8 * tm * mv * 4)
    return pl.pallas_call(
        kern,
        grid=(b, l // tm),
        in_specs=[
            pl.BlockSpec((1, tm, d), row),
            _resident((1, d)),
            _resident(w_main.shape),
            _resident(w_gate.shape),
            _resident((1, V7X_LANES)),
            _resident((t, t)),
            _resident((t, t)),
        ],
        out_specs=[
            pl.BlockSpec((1, tm, mq), row),
            pl.BlockSpec((1, tm, mq), row),
            pl.BlockSpec((1, mq, tm), col),
            pl.BlockSpec((1, mq, tm), col),
            pl.BlockSpec((1, tm, mv), row),
            pl.BlockSpec((1, tm, V7X_LANES), row),
            pl.BlockSpec((1, ng, tm), col),
        ],
        out_shape=[
            jax.ShapeDtypeStruct((b, l, mq), BF16),
            jax.ShapeDtypeStruct((b, l, mq), BF16),
            jax.ShapeDtypeStruct((b, mq, l), BF16),
            jax.ShapeDtypeStruct((b, mq, l), BF16),
            jax.ShapeDtypeStruct((b, l, mv), BF16),
            jax.ShapeDtypeStruct((b, l, V7X_LANES), F32),
            jax.ShapeDtypeStruct((b, ng, l), F32),
        ],
        compiler_params=pltpu.CompilerParams(
            dimension_semantics=("parallel", "parallel"), vmem_limit_bytes=_vmem_limit(vmem)),
        name="mlstm_proj",
    )(x, norm_g.reshape(1, d), w_main, w_gate, b_pad, pre_m, suf_m)


def _scan_kernel(qf_ref, kf_ref, qtf_ref, ktf_ref, vf_ref, gcf_ref, grf_ref,
                 qb_ref, kb_ref, qtb_ref, ktb_ref, vb_ref, gcb_ref, grb_ref,
                 hf_ref, hb_ref, c_ref, rhs_ref, n_ref, m_ref, *, t, heads, dqk, dv):
    step = pl.program_id(1)
    par = step % 2

    @pl.when(step == 0)
    def _():
        c_ref[...] = jnp.zeros_like(c_ref)
        rhs_ref[...] = jnp.zeros_like(rhs_ref)
        n_ref[...] = jnp.zeros_like(n_ref)
        m_ref[...] = jnp.zeros_like(m_ref)

    src_i = lax.broadcasted_iota(jnp.int32, (t, t), 0)
    tgt_i = lax.broadcasted_iota(jnp.int32, (t, t), 1)
    first_head = lax.broadcasted_iota(jnp.int32, (t, 2 * dqk), 1) < dqk
    n_row = lax.broadcasted_iota(jnp.int32, (N_ROWS, 2 * t), 0)
    n_lane_first = lax.broadcasted_iota(jnp.int32, (N_ROWS, 2 * t), 1) < t
    own_block = ((n_row == 0) & n_lane_first) | ((n_row == 1) & jnp.logical_not(n_lane_first))
    n_row_q = lax.broadcasted_iota(jnp.int32, (N_ROWS, 2 * dqk), 0)
    fwd_refs = (qf_ref, kf_ref, qtf_ref, ktf_ref, vf_ref, gcf_ref, grf_ref, hf_ref)
    bwd_refs = (qb_ref, kb_ref, qtb_ref, ktb_ref, vb_ref, gcb_ref, grb_ref, hb_ref)
    streams = []
    for bb in range(hf_ref.shape[0]):
        streams.append(tuple(r.at[bb:bb + 1] for r in fwd_refs) + (src_i <= tgt_i, 0, t - 1))
        streams.append(tuple(r.at[bb:bb + 1] for r in bwd_refs) + (src_i >= tgt_i, 2 * heads, 0))
    dn_nt = (((1,), (1,)), ((), ()))
    dn_tn = (((0,), (0,)), ((), ()))
    half = heads // 2

    def block_diag(a2):
        zero = jnp.zeros_like(a2)
        return jnp.concatenate([jnp.where(first_head, a2, zero), jnp.where(first_head, zero, a2)], axis=0)

    every = [(di, h, di * heads + h) for di in range(len(streams)) for h in range(heads)]
    pairs = [(di, p, di * half + p) for di in range(len(streams)) for p in range(half)]
    st = {}
    for di, h, idx in every:
        gc_ref, gr_ref, base, end = streams[di][5], streams[di][6], streams[di][9], streams[di][10]
        ig_r = gr_ref[0, base + h:base + h + 1, :]
        b_r = gr_ref[0, base + heads + h:base + heads + h + 1, :]
        total = b_r[:, end:end + 1]
        m_prev = m_ref[idx]
        st[idx] = dict(b_r=b_r, inter=b_r + m_prev, gk=total - b_r + ig_r, carry=total + m_prev,
                       src_c=gc_ref[0, :, base + h:base + h + 1] - gc_ref[0, :, base + heads + h:base + heads + h + 1])
    for di, h, idx in every:
        e = st[idx]
        m_new = jnp.maximum(e["carry"], jnp.max(e["gk"], axis=-1, keepdims=True))
        m_ref[idx] = m_new
        e["wk"] = jnp.exp(e["gk"] - m_new)
        e["decay"] = jnp.exp(e["carry"] - m_new)

    sp = {}
    for di, p, pi in pairs:
        sl = slice(2 * p * dqk, (2 * p + 2) * dqk)
        sp[pi] = dict(k2=streams[di][1][0, :, sl], n_prev=n_ref[pi])
        lhs = jnp.concatenate([sp[pi]["k2"], sp[pi]["n_prev"].astype(BF16)], axis=0)
        s2 = lax.dot_general(lhs, block_diag(streams[di][0][0, :, sl]), dn_nt, preferred_element_type=F32)
        for j in range(2):
            e = st[di * heads + 2 * p + j]
            e["qk"] = s2[:t, j * t:(j + 1) * t]
            e["qn"] = s2[t + j:t + j + 1, j * t:(j + 1) * t]

    for di, h, idx in every:
        e = st[idx]
        e["v"] = streams[di][4][0, :, h * dv:(h + 1) * dv]
        e["kwt"] = (streams[di][3][0, h * dqk:(h + 1) * dqk, :].astype(F32) * e["wk"]).astype(BF16)
    for di, h, idx in every:
        e = st[idx]
        e["upd"] = jnp.dot(e["kwt"], e["v"], preferred_element_type=F32)
        rhs_ref[par, idx, :t, :] = e["v"]
    for di, p, pi in pairs:
        ea, eb = st[di * heads + 2 * p], st[di * heads + 2 * p + 1]
        wk2 = jnp.concatenate([ea["wk"], eb["wk"]], axis=1)
        w_rows = jnp.where(own_block, wk2, 0.0).astype(BF16)
        dec = jnp.where(n_row_q == 0, ea["decay"], eb["decay"])
        n_ref[pi] = dec * sp[pi]["n_prev"] + jnp.dot(w_rows, block_diag(sp[pi]["k2"]), preferred_element_type=F32)
    for di, h, idx in every:
        e = st[idx]
        c_new = e["decay"] * c_ref[idx] + e["upd"]
        c_ref[idx] = c_new
        rhs_ref[1 - par, idx, t:, :] = c_new.astype(BF16)

    for di, h, idx in every:
        e = st[idx]
        e["dlog"] = jnp.where(streams[di][8], e["src_c"] + e["b_r"], -jnp.inf)
    for di, h, idx in every:
        e = st[idx]
        e["m_t"] = jnp.maximum(e["inter"], jnp.max(e["dlog"], axis=0, keepdims=True))
    for di, h, idx in every:
        e = st[idx]
        e["s"] = e["qk"] * jnp.exp(e["dlog"] - e["m_t"])
        e["sc"] = jnp.exp(e["inter"] - e["m_t"])
    for di, h, idx in every:
        e = st[idx]
        den = jnp.sum(e["s"], axis=0, keepdims=True) + e["sc"] * e["qn"]
        e["r"] = 1.0 / jnp.maximum(jnp.abs(den), jnp.exp(-e["m_t"]))
    for di, h, idx in every:
        e = st[idx]
        qt32 = streams[di][2][0, h * dqk:(h + 1) * dqk, :].astype(F32)
        e["lhs_t"] = jnp.concatenate(
            [(e["s"] * e["r"]).astype(BF16), (qt32 * (e["sc"] * e["r"])).astype(BF16)], axis=0)
    for di, h, idx in every:
        streams[di][7][0, :, h * dv:(h + 1) * dv] = lax.dot_general(
            st[idx]["lhs_t"], rhs_ref[par, idx], dn_tn, preferred_element_type=F32).astype(streams[di][7].dtype)


def _mlstm_scan(q, k, qt, kt, v, gc, gr, *, heads):
    b, l, mq = q.shape
    mv = v.shape[-1]
    dqk, dv = mq // heads, mv // heads
    t = min(SCAN_CHUNK, l)
    assert l % t == 0 and heads % 2 == 0
    nc = l // t
    ng = gr.shape[1]
    kern = functools.partial(_scan_kernel, t=t, heads=heads, dqk=dqk, dv=dv)
    fwd = lambda bi, c: (bi, c, 0)
    bwd = lambda bi, c: (bi, nc - 1 - c, 0)
    fwd_r = lambda bi, c: (bi, 0, c)
    bwd_r = lambda bi, c: (bi, 0, nc - 1 - c)

    rows = SCAN_ROWS if b % SCAN_ROWS == 0 else 1
    chains = 2 * rows * heads

    def stream(tok, rowm):
        return [pl.BlockSpec((rows, t, mq), tok), pl.BlockSpec((rows, t, mq), tok),
                pl.BlockSpec((rows, mq, t), rowm), pl.BlockSpec((rows, mq, t), rowm),
                pl.BlockSpec((rows, t, mv), tok),
                pl.BlockSpec((rows, t, V7X_LANES), tok), pl.BlockSpec((rows, ng, t), rowm)]

    return pl.pallas_call(
        kern,
        grid=(b // rows, nc),
        in_specs=stream(fwd, fwd_r) + stream(bwd, bwd_r),
        out_specs=[pl.BlockSpec((rows, t, mv), fwd), pl.BlockSpec((rows, t, mv), bwd)],
        out_shape=[jax.ShapeDtypeStruct((b, l, mv), BF16), jax.ShapeDtypeStruct((b, l, mv), BF16)],
        scratch_shapes=[pltpu.VMEM((chains, dqk, dv), F32), pltpu.VMEM((2, chains, t + dqk, dv), BF16),
                        pltpu.VMEM((chains // 2, N_ROWS, 2 * dqk), F32), pltpu.VMEM((chains, 1, 1), F32)],
        compiler_params=pltpu.CompilerParams(dimension_semantics=("parallel", "arbitrary")),
        name="mlstm_scan",
    )(q, k, qt, kt, v, gc, gr, q, k, qt, kt, v, gc, gr)


def _mout_kernel(hf_ref, hb_ref, x_ref, g_ref, wo_ref, hg_ref, w_ref, out_ref, y_ref, *, heads, dv):
    x = x_ref[0]
    o = jnp.dot(_rmsnorm(x, g_ref[...]).astype(BF16), wo_ref[...], preferred_element_type=F32)
    for h in range(heads):
        sl = slice(h * dv, (h + 1) * dv)
        hs = hf_ref[0, :, sl].astype(F32) + hb_ref[0, :, sl].astype(F32)
        hs = hs * lax.rsqrt(jnp.mean(hs * hs, axis=-1, keepdims=True) + EPS)
        hs = hs * hg_ref[:, sl]
        y_ref[:, sl] = (hs * jax.nn.sigmoid(o[:, sl])).astype(BF16)
    out_ref[0] = x + jnp.dot(y_ref[...], w_ref[...], preferred_element_type=F32)


def _mlstm_out(hf, hb, x, norm_g, w_gate_out, head_g, w_out):
    b, l, d = x.shape
    heads, dv = head_g.shape
    mv = heads * dv
    tm = min(1024, l)
    assert l % tm == 0
    row = lambda bi, i: (bi, i, 0)
    w = w_out.astype(BF16)
    wo = w_gate_out.astype(BF16)
    vmem = (w.size + wo.size) * 2 + 2 * tm * (2 * mv * 2 + 2 * d * 4) + tm * mv * 2 + 8 * tm * d * 4
    return pl.pallas_call(
        functools.partial(_mout_kernel, heads=heads, dv=dv),
        grid=(b, l // tm),
        in_specs=[pl.BlockSpec((1, tm, mv), row), pl.BlockSpec((1, tm, mv), row), pl.BlockSpec((1, tm, d), row),
                  _resident((1, d)), _resident(wo.shape), _resident((1, mv)), _resident(w.shape)],
        out_specs=pl.BlockSpec((1, tm, d), row),
        out_shape=jax.ShapeDtypeStruct((b, l, d), F32),
        scratch_shapes=[pltpu.VMEM((tm, mv), BF16)],
        compiler_params=pltpu.CompilerParams(
            dimension_semantics=("parallel", "parallel"), vmem_limit_bytes=_vmem_limit(vmem)),
        name="mlstm_out",
    )(hf, hb, x, norm_g.reshape(1, d), wo, head_g.reshape(1, mv), w)


def _dft_tables(l1, l2, gd, n2h, n2b):
    l = l1 * l2
    a1 = 2.0 * np.pi * np.outer(np.arange(l1), np.arange(l1)) / l1
    wa = np.stack([np.cos(a1), -np.sin(a1)]) / np.sqrt(l1)
    wa = np.einsum("rkn,ij->krinj", wa, np.eye(n2h)).reshape(l1 * 2 * n2h, l1 * n2h)
    tw = 2.0 * np.pi * np.einsum("k,jhi->jhki", np.arange(l1),
                                 np.arange(l2).reshape(l2 // n2b, n2b // n2h, n2h)) / l
    twr, twi = np.cos(tw)[..., None], -np.sin(tw)[..., None]
    a2 = 2.0 * np.pi * np.outer(np.arange(l2), np.arange(l2)) / l2
    c2, s2 = np.cos(a2), np.sin(a2)
    wc = np.block([[c2, s2], [-s2, c2]]) / np.sqrt(l2)
    ac = 2.0 * np.pi * np.outer(np.arange(gd), np.arange(gd)) / gd
    cs = np.stack([np.cos(ac), np.sin(ac)]) / np.sqrt(gd)
    return (jnp.asarray(wa, BF16), jnp.asarray(twr, F32), jnp.asarray(twi, F32),
            jnp.asarray(wc, BF16), jnp.asarray(cs, BF16))


def _fa_kernel(x_ref, g_ref, wa_ref, twr_ref, twi_ref, out_ref, *, l1, n2h, n2b, d):
    g = g_ref[...]
    wa = wa_ref[...]
    halves = []
    for hh in range(n2b // n2h):
        x = x_ref[0, :, hh * n2h:(hh + 1) * n2h, :].reshape(l1 * n2h, d)
        a = jnp.dot(wa, _rmsnorm(x, g).astype(BF16), preferred_element_type=F32)
        a = a.reshape(l1, 2, n2h, d)
        ar = a[:, 0]
        ai = a[:, 1]
        tr = twr_ref[0, hh]
        ti = twi_ref[0, hh]
        halves.append(jnp.stack([ar * tr - ai * ti, ar * ti + ai * tr], axis=1))
    out_ref[0] = jnp.concatenate(halves, axis=2).astype(BF16)


def _fc_kernel(ba_ref, x_ref, wc_ref, cs_ref, wout_ref, bout_ref, out_ref, z_ref, f_ref, *, l2, k1b, d, gd):
    wc = wc_ref[...]
    for j in range(k1b):
        z = jnp.dot(wc, ba_ref[0, j], preferred_element_type=F32)
        z_ref[0, j * l2:(j + 1) * l2, :] = z[:l2].astype(BF16)
        z_ref[1, j * l2:(j + 1) * l2, :] = z[l2:].astype(BF16)
    for gi in range(d // gd):
        sl = slice(gi * gd, (gi + 1) * gd)
        f = (jnp.dot(z_ref[0, :, sl], cs_ref[0], preferred_element_type=F32)
             + jnp.dot(z_ref[1, :, sl], cs_ref[1], preferred_element_type=F32))
        f_ref[:, sl] = f.astype(BF16)
    y = jnp.dot(f_ref[...], wout_ref[...], preferred_element_type=F32) + bout_ref[...]
    for j in range(k1b):
        out_ref[0, :, j, :] = x_ref[0, :, j, :] + y[j * l2:(j + 1) * l2]


def _fourier_mixer(x, norm_g, w_out, b_out):
    b, l, d = x.shape
    l2 = min(DFT_INNER, l)
    l1 = l // l2
    assert l1 * l2 == l and d % F_GROUPS == 0
    gd = d // F_GROUPS
    n2h = V7X_SUBLANES
    n2b = min(V7X_BF16_ROWS, l2)
    k1b = min(V7X_SUBLANES, l1)
    assert l2 % n2b == 0 and n2b % n2h == 0 and l1 % k1b == 0
    wa, twr, twi, wc, cs = _dft_tables(l1, l2, gd, n2h, n2b)

    tw_spec = pl.BlockSpec((1, n2b // n2h, l1, n2h, 1), lambda bi, j: (j, 0, 0, 0, 0))
    ba = pl.pallas_call(
        functools.partial(_fa_kernel, l1=l1, n2h=n2h, n2b=n2b, d=d),
        grid=(b, l2 // n2b),
        in_specs=[pl.BlockSpec((1, l1, n2b, d), lambda bi, j: (bi, 0, j, 0)),
                  _resident((1, d)), _resident(wa.shape), tw_spec, tw_spec],
        out_specs=pl.BlockSpec((1, l1, 2, n2b, d), lambda bi, j: (bi, 0, 0, j, 0)),
        out_shape=jax.ShapeDtypeStruct((b, l1, 2, l2, d), BF16),
        compiler_params=pltpu.CompilerParams(
            dimension_semantics=("parallel", "parallel"),
            vmem_limit_bytes=_vmem_limit(l1 * n2b * d * (2 * 4 + 2 * 2 * 2 + 6 * 4) + wa.size * 2)),
        name="fourier_seq_a",
    )(x.reshape(b, l1, l2, d), norm_g.reshape(1, d), wa, twr, twi)

    rows = k1b * l2
    out = pl.pallas_call(
        functools.partial(_fc_kernel, l2=l2, k1b=k1b, d=d, gd=gd),
        grid=(b, l1 // k1b),
        in_specs=[pl.BlockSpec((1, k1b, 2 * l2, d), lambda bi, j: (bi, j, 0, 0)),
                  pl.BlockSpec((1, l2, k1b, d), lambda bi, j: (bi, 0, j, 0)),
                  _resident(wc.shape), _resident(cs.shape), _resident((d, d)), _resident((1, d))],
        out_specs=pl.BlockSpec((1, l2, k1b, d), lambda bi, j: (bi, 0, j, 0)),
        out_shape=jax.ShapeDtypeStruct((b, l2, l1, d), F32),
        scratch_shapes=[pltpu.VMEM((2, rows, d), BF16), pltpu.VMEM((rows, d), BF16)],
        compiler_params=pltpu.CompilerParams(
            dimension_semantics=("parallel", "parallel"),
            vmem_limit_bytes=_vmem_limit(rows * d * (2 * 2 * 2 + 4 * 4 + 3 * 2 + 4 * 4) + 4 * d * d)),
        name="fourier_seq_c",
    )(ba.reshape(b, l1, 2 * l2, d), x.reshape(b, l2, l1, d), wc, cs, w_out.astype(BF16), b_out.reshape(1, d))
    return out.reshape(b, l, d)


def _mlstm_mixer(x, norm_g, w_in, b_gate, head_g, w_out):
    heads, dv = head_g.shape
    mv = heads * dv
    q, k, qt, kt, v, gc, gr = _mlstm_proj(x, norm_g, w_in, b_gate, heads=heads, mv=mv)
    hf, hb = _mlstm_scan(q, k, qt, kt, v, gc, gr, heads=heads)
    o_lo = 2 * q.shape[-1] + mv
    return _mlstm_out(hf, hb, x, norm_g, w_in[:, o_lo:o_lo + mv], head_g, w_out)


def _trunk(x, mix_norm_g, m_w_in, m_b_gate, m_head_g, m_w_out, f_w_out, f_b_out,
           ffn_norm_g, ffn_w_up, ffn_conv_w, ffn_conv_b, ffn_w_down, final_norm_g):
    depth = mix_norm_g.shape[0]
    n_mixers = 2
    for i in range(depth):
        j = i // n_mixers
        if i % n_mixers == 0:
            x = _mlstm_mixer(x, mix_norm_g[i], m_w_in[j], m_b_gate[j], m_head_g[j], m_w_out[j])
        else:
            x = _fourier_mixer(x, mix_norm_g[i], f_w_out[j], f_b_out[j])
        x = _conv_ffn(x, ffn_norm_g[i], ffn_w_up[i], ffn_conv_w[i], ffn_conv_b[i], ffn_w_down[i],
                      final_norm_g, final_norm=(i == depth - 1))
    return x


def kernel(x_prompt, x_sample, mix_norm_g, m_w_in, m_b_gate, m_head_g, m_w_out, f_w_out, f_b_out,
           ffn_norm_g, ffn_w_up, ffn_conv_w, ffn_conv_b, ffn_w_down, final_norm_g):
    params = (mix_norm_g, m_w_in, m_b_gate, m_head_g, m_w_out, f_w_out, f_b_out,
              ffn_norm_g, ffn_w_up, ffn_conv_w, ffn_conv_b, ffn_w_down, final_norm_g)
    return (_trunk(x_prompt, *params), _trunk(x_sample, *params))
```

```python
import functools

import numpy as np
import jax
import jax.numpy as jnp
from jax import lax
from jax.experimental import pallas as pl
from jax.experimental.pallas import tpu as pltpu

F32 = jnp.float32
BF16 = jnp.bfloat16

EPS = 1e-6
F_GROUPS = 4
N_GATE_KINDS = 4

V7X_VMEM_BYTES = 64 * 1024 * 1024
V7X_LANES = 128
V7X_SUBLANES = 8
V7X_BF16_ROWS = 16
V7X_MXU_DIM = 256

HALO = V7X_SUBLANES
SCAN_CHUNK = 128
N_ROWS = V7X_BF16_ROWS
SCAN_ROWS = 2
DFT_INNER = 128


def _vmem_limit(nbytes):
    return int(min(max(nbytes, 32 * 1024 * 1024), V7X_VMEM_BYTES - 8 * 1024 * 1024))


def _rmsnorm(x, g):
    y = x * lax.rsqrt(jnp.mean(x * x, axis=-1, keepdims=True) + EPS)
    return y * g


def _resident(shape):
    nd = len(shape)
    return pl.BlockSpec(shape, lambda *_: (0,) * nd, pipeline_mode=pl.Buffered(1))


def _ffn_kernel(xm_ref, xp_ref, xn_ref, g_ref, wup_ref, cw_ref, cb_ref, wdn_ref, fg_ref,
                o_ref, h_ref, act_ref, *, tm, n_chunks, fc, conv_w, final_norm):
    i = pl.program_id(1)
    last = pl.num_programs(1) - 1
    g = g_ref[...]
    x = xm_ref[0]
    hp = jnp.where(i > 0, _rmsnorm(xp_ref[0], g), 0.0)
    hn = jnp.where(i < last, _rmsnorm(xn_ref[0], g), 0.0)
    h_ref[0:HALO, :] = hp.astype(BF16)
    h_ref[HALO:HALO + tm, :] = _rmsnorm(x, g).astype(BF16)
    h_ref[HALO + tm:, :] = hn.astype(BF16)
    hb = h_ref[...]
    pad = conv_w // 2
    d_ff = n_chunks * fc

    def conv_cols(lo):
        u = jnp.dot(hb, wup_ref[:, lo:lo + fc], preferred_element_type=F32)
        y = cb_ref[:, lo:lo + fc]
        for j in [pad] + [j for j in range(conv_w) if j != pad]:
            off = HALO - pad + j
            y = y + u[off:off + tm] * cw_ref[j:j + 1, lo:lo + fc]
        return y

    for c in range(n_chunks):
        a = conv_cols(c * fc)
        val = conv_cols(d_ff + c * fc)
        act_ref[:, c * fc:(c + 1) * fc] = (a * jax.nn.sigmoid(a) * val).astype(BF16)
    r = x + jnp.dot(act_ref[...], wdn_ref[...], preferred_element_type=F32)
    if final_norm:
        r = _rmsnorm(r, fg_ref[...])
    o_ref[0] = r


def _conv_ffn(x, norm_g, w_up, conv_w, conv_b, w_down, final_g, *, final_norm):
    b, l, d = x.shape
    d_ff = w_down.shape[0]
    cw = conv_w.shape[0]
    fc = V7X_MXU_DIM
    n_chunks = d_ff // fc
    assert n_chunks * fc == d_ff and cw // 2 <= HALO
    tm = min(1024, l)
    assert l % tm == 0 and tm % HALO == 0
    nt = l // tm
    rb = tm // HALO
    assert (2 * d_ff) % V7X_LANES == 0 and fc % V7X_LANES == 0
    wup = w_up.astype(BF16)
    cwr = conv_w
    cbr = conv_b.reshape(1, 2 * d_ff)
    wdn = w_down.astype(BF16)
    kern = functools.partial(_ffn_kernel, tm=tm, n_chunks=n_chunks, fc=fc, conv_w=cw, final_norm=final_norm)
    vmem = (wup.size * 2 + wdn.size * 2 + 4 * tm * d * 4 + (tm + 2 * HALO) * d * 2 + tm * d_ff * 2
            + 8 * (tm + 2 * HALO) * fc * 4 + 3 * tm * d * 4)
    return pl.pallas_call(
        kern,
        grid=(b, nt),
        in_specs=[
            pl.BlockSpec((1, tm, d), lambda bi, i: (bi, i, 0)),
            pl.BlockSpec((1, HALO, d), lambda bi, i: (bi, jnp.maximum(i * rb - 1, 0), 0)),
            pl.BlockSpec((1, HALO, d), lambda bi, i: (bi, jnp.minimum((i + 1) * rb, l // HALO - 1), 0)),
            _resident((1, d)),
            _resident(wup.shape),
            _resident(cwr.shape),
            _resident(cbr.shape),
            _resident(wdn.shape),
            _resident((1, d)),
        ],
        out_specs=pl.BlockSpec((1, tm, d), lambda bi, i: (bi, i, 0)),
        out_shape=jax.ShapeDtypeStruct((b, l, d), F32),
        scratch_shapes=[pltpu.VMEM((tm + 2 * HALO, d), BF16), pltpu.VMEM((tm, d_ff), BF16)],
        compiler_params=pltpu.CompilerParams(
            dimension_semantics=("parallel", "arbitrary"), vmem_limit_bytes=_vmem_limit(vmem)),
        name="conv_ffn",
    )(x, x, x, norm_g.reshape(1, d), wup, cwr, cbr, wdn, final_g.reshape(1, d))


def _split3(x):
    hi = x.astype(BF16)
    r = x - hi.astype(F32)
    mid = r.astype(BF16)
    lo = (r - mid.astype(F32)).astype(BF16)
    return hi, mid, lo


def _proj_kernel(x_ref, g_ref, w_ref, wg_ref, bg_ref, pre_ref, suf_ref,
                 q_ref, k_ref, qt_ref, kt_ref, v_ref, gc_ref, gr_ref, *, mq, mv, heads, k_scale):
    h = _rmsnorm(x_ref[0], g_ref[...]).astype(BF16)
    q = jnp.dot(h, w_ref[:, 0:mq], preferred_element_type=F32)
    k = jnp.dot(h, w_ref[:, mq:2 * mq], preferred_element_type=F32) * k_scale
    q_ref[0] = q.astype(BF16)
    k_ref[0] = k.astype(BF16)
    qt_ref[0] = q.T.astype(BF16)
    kt_ref[0] = k.T.astype(BF16)
    v_ref[0] = jnp.dot(h, w_ref[:, 2 * mq:2 * mq + mv], preferred_element_type=F32).astype(BF16)
    gt = jnp.dot(h, wg_ref[...], preferred_element_type=F32) + bg_ref[...]
    lane = lax.broadcasted_iota(jnp.int32, gt.shape, 1)
    kind = lane // heads
    log_sig = jnp.minimum(gt, 0.0) - jnp.log1p(jnp.exp(-jnp.abs(gt)))
    parts = _split3(log_sig)
    t = pre_ref.shape[0]
    pre, suf = [], []
    for c in range(gt.shape[0] // t):
        rows = slice(c * t, (c + 1) * t)
        pre.append(sum(jnp.dot(pre_ref[...], p[rows], preferred_element_type=F32) for p in parts))
        suf.append(sum(jnp.dot(suf_ref[...], p[rows], preferred_element_type=F32) for p in parts))
    gt = jnp.where(kind == 1, jnp.concatenate(pre, axis=0), jnp.where(kind == 3, jnp.concatenate(suf, axis=0), gt))
    gc_ref[0] = gt
    gr_ref[0] = gt.T[0:gr_ref.shape[1], :]


def _mlstm_proj(x, norm_g, w_in, b_gate, *, heads, mv):
    b, l, d = x.shape
    ng = N_GATE_KINDS * heads
    mq = (w_in.shape[1] - 2 * mv - ng) // 2
    dqk = mq // heads
    tm = min(1024, l)
    assert l % tm == 0 and ng <= V7X_LANES
    w_main = w_in[:, :2 * mq + mv].astype(BF16)
    w_gate = jnp.pad(w_in[:, 2 * mq + 2 * mv:], ((0, 0), (0, V7X_LANES - ng))).astype(BF16)
    b_pad = jnp.pad(b_gate, (0, V7X_LANES - ng)).reshape(1, V7X_LANES)
    t = min(SCAN_CHUNK, l)
    assert tm % t == 0
    pos = np.arange(t)
    pre_m = jnp.asarray(pos[None, :] <= pos[:, None], BF16)
    suf_m = jnp.asarray(pos[None, :] >= pos[:, None], BF16)
    kern = functools.partial(_proj_kernel, mq=mq, mv=mv, heads=heads, k_scale=float(dqk) ** -0.5)
    row = lambda bi, i: (bi, i, 0)
    col = lambda bi, i: (bi, 0, i)
    vmem = (w_main.size * 2 + w_gate.size * 2 + 2 * tm * d * 4 + 2 * tm * (4 * mq + mv) * 2 + 8 * tm * mv * 4)
    return pl.pallas_call(
        kern,
        grid=(b, l // tm),
        in_specs=[
            pl.BlockSpec((1, tm, d), row),
            _resident((1, d)),
            _resident(w_main.shape),
            _resident(w_gate.shape),
            _resident((1, V7X_LANES)),
            _resident((t, t)),
            _resident((t, t)),
        ],
        out_specs=[
            pl.BlockSpec((1, tm, mq), row),
            pl.BlockSpec((1, tm, mq), row),
            pl.BlockSpec((1, mq, tm), col),
            pl.BlockSpec((1, mq, tm), col),
            pl.BlockSpec((1, tm, mv), row),
            pl.BlockSpec((1, tm, V7X_LANES), row),
            pl.BlockSpec((1, ng, tm), col),
        ],
        out_shape=[
            jax.ShapeDtypeStruct((b, l, mq), BF16),
            jax.ShapeDtypeStruct((b, l, mq), BF16),
            jax.ShapeDtypeStruct((b, mq, l), BF16),
            jax.ShapeDtypeStruct((b, mq, l), BF16),
            jax.ShapeDtypeStruct((b, l, mv), BF16),
            jax.ShapeDtypeStruct((b, l, V7X_LANES), F32),
            jax.ShapeDtypeStruct((b, ng, l), F32),
        ],
        compiler_params=pltpu.CompilerParams(
            dimension_semantics=("parallel", "parallel"), vmem_limit_bytes=_vmem_limit(vmem)),
        name="mlstm_proj",
    )(x, norm_g.reshape(1, d), w_main, w_gate, b_pad, pre_m, suf_m)


def _scan_kernel(qf_ref, kf_ref, qtf_ref, ktf_ref, vf_ref, gcf_ref, grf_ref,
                 qb_ref, kb_ref, qtb_ref, ktb_ref, vb_ref, gcb_ref, grb_ref,
                 hf_ref, hb_ref, c_ref, rhs_ref, n_ref, m_ref, *, t, heads, dqk, dv):
    step = pl.program_id(1)
    par = step % 2

    @pl.when(step == 0)
    def _():
        c_ref[...] = jnp.zeros_like(c_ref)
        rhs_ref[...] = jnp.zeros_like(rhs_ref)
        n_ref[...] = jnp.zeros_like(n_ref)
        m_ref[...] = jnp.zeros_like(m_ref)

    src_i = lax.broadcasted_iota(jnp.int32, (t, t), 0)
    tgt_i = lax.broadcasted_iota(jnp.int32, (t, t), 1)
    first_head = lax.broadcasted_iota(jnp.int32, (t, 2 * dqk), 1) < dqk
    n_row = lax.broadcasted_iota(jnp.int32, (N_ROWS, 2 * t), 0)
    n_lane_first = lax.broadcasted_iota(jnp.int32, (N_ROWS, 2 * t), 1) < t
    own_block = ((n_row == 0) & n_lane_first) | ((n_row == 1) & jnp.logical_not(n_lane_first))
    n_row_q = lax.broadcasted_iota(jnp.int32, (N_ROWS, 2 * dqk), 0)
    fwd_refs = (qf_ref, kf_ref, qtf_ref, ktf_ref, vf_ref, gcf_ref, grf_ref, hf_ref)
    bwd_refs = (qb_ref, kb_ref, qtb_ref, ktb_ref, vb_ref, gcb_ref, grb_ref, hb_ref)
    streams = []
    for bb in range(hf_ref.shape[0]):
        streams.append(tuple(r.at[bb:bb + 1] for r in fwd_refs) + (src_i <= tgt_i, 0, t - 1))
        streams.append(tuple(r.at[bb:bb + 1] for r in bwd_refs) + (src_i >= tgt_i, 2 * heads, 0))
    dn_nt = (((1,), (1,)), ((), ()))
    dn_tn = (((0,), (0,)), ((), ()))
    half = heads // 2

    def block_diag(a2):
        zero = jnp.zeros_like(a2)
        return jnp.concatenate([jnp.where(first_head, a2, zero), jnp.where(first_head, zero, a2)], axis=0)

    every = [(di, h, di * heads + h) for di in range(len(streams)) for h in range(heads)]
    pairs = [(di, p, di * half + p) for di in range(len(streams)) for p in range(half)]
    st = {}
    for di, h, idx in every:
        gc_ref, gr_ref, base, end = streams[di][5], streams[di][6], streams[di][9], streams[di][10]
        ig_r = gr_ref[0, base + h:base + h + 1, :]
        b_r = gr_ref[0, base + heads + h:base + heads + h + 1, :]
        total = b_r[:, end:end + 1]
        m_prev = m_ref[idx]
        st[idx] = dict(b_r=b_r, inter=b_r + m_prev, gk=total - b_r + ig_r, carry=total + m_prev,
                       src_c=gc_ref[0, :, base + h:base + h + 1] - gc_ref[0, :, base + heads + h:base + heads + h + 1])
    for di, h, idx in every:
        e = st[idx]
        m_new = jnp.maximum(e["carry"], jnp.max(e["gk"], axis=-1, keepdims=True))
        m_ref[idx] = m_new
        e["wk"] = jnp.exp(e["gk"] - m_new)
        e["decay"] = jnp.exp(e["carry"] - m_new)

    sp = {}
    for di, p, pi in pairs:
        sl = slice(2 * p * dqk, (2 * p + 2) * dqk)
        sp[pi] = dict(k2=streams[di][1][0, :, sl], n_prev=n_ref[pi])
        lhs = jnp.concatenate([sp[pi]["k2"], sp[pi]["n_prev"].astype(BF16)], axis=0)
        s2 = lax.dot_general(lhs, block_diag(streams[di][0][0, :, sl]), dn_nt, preferred_element_type=F32)
        for j in range(2):
            e = st[di * heads + 2 * p + j]
            e["qk"] = s2[:t, j * t:(j + 1) * t]
            e["qn"] = s2[t + j:t + j + 1, j * t:(j + 1) * t]

    for di, h, idx in every:
        e = st[idx]
        e["v"] = streams[di][4][0, :, h * dv:(h + 1) * dv]
        e["kwt"] = (streams[di][3][0, h * dqk:(h + 1) * dqk, :].astype(F32) * e["wk"]).astype(BF16)
    for di, h, idx in every:
        e = st[idx]
        e["upd"] = jnp.dot(e["kwt"], e["v"], preferred_element_type=F32)
        rhs_ref[par, idx, :t, :] = e["v"]
    for di, p, pi in pairs:
        ea, eb = st[di * heads + 2 * p], st[di * heads + 2 * p + 1]
        wk2 = jnp.concatenate([ea["wk"], eb["wk"]], axis=1)
        w_rows = jnp.where(own_block, wk2, 0.0).astype(BF16)
        dec = jnp.where(n_row_q == 0, ea["decay"], eb["decay"])
        n_ref[pi] = dec * sp[pi]["n_prev"] + jnp.dot(w_rows, block_diag(sp[pi]["k2"]), preferred_element_type=F32)
    for di, h, idx in every:
        e = st[idx]
        c_new = e["decay"] * c_ref[idx] + e["upd"]
        c_ref[idx] = c_new
        rhs_ref[1 - par, idx, t:, :] = c_new.astype(BF16)

    for di, h, idx in every:
        e = st[idx]
        e["dlog"] = jnp.where(streams[di][8], e["src_c"] + e["b_r"], -jnp.inf)
    for di, h, idx in every:
        e = st[idx]
        e["m_t"] = jnp.maximum(e["inter"], jnp.max(e["dlog"], axis=0, keepdims=True))
    for di, h, idx in every:
        e = st[idx]
        e["s"] = e["qk"] * jnp.exp(e["dlog"] - e["m_t"])
        e["sc"] = jnp.exp(e["inter"] - e["m_t"])
    for di, h, idx in every:
        e = st[idx]
        den = jnp.sum(e["s"], axis=0, keepdims=True) + e["sc"] * e["qn"]
        e["r"] = 1.0 / jnp.maximum(jnp.abs(den), jnp.exp(-e["m_t"]))
    for di, h, idx in every:
        e = st[idx]
        qt32 = streams[di][2][0, h * dqk:(h + 1) * dqk, :].astype(F32)
        e["lhs_t"] = jnp.concatenate(
            [(e["s"] * e["r"]).astype(BF16), (qt32 * (e["sc"] * e["r"])).astype(BF16)], axis=0)
    for di, h, idx in every:
        streams[di][7][0, :, h * dv:(h + 1) * dv] = lax.dot_general(
            st[idx]["lhs_t"], rhs_ref[par, idx], dn_tn, preferred_element_type=F32).astype(streams[di][7].dtype)


def _mlstm_scan(q, k, qt, kt, v, gc, gr, *, heads):
    b, l, mq = q.shape
    mv = v.shape[-1]
    dqk, dv = mq // heads, mv // heads
    t = min(SCAN_CHUNK, l)
    assert l % t == 0 and heads % 2 == 0
    nc = l // t
    ng = gr.shape[1]
    kern = functools.partial(_scan_kernel, t=t, heads=heads, dqk=dqk, dv=dv)
    fwd = lambda bi, c: (bi, c, 0)
    bwd = lambda bi, c: (bi, nc - 1 - c, 0)
    fwd_r = lambda bi, c: (bi, 0, c)
    bwd_r = lambda bi, c: (bi, 0, nc - 1 - c)

    rows = SCAN_ROWS if b % SCAN_ROWS == 0 else 1
    chains = 2 * rows * heads

    def stream(tok, rowm):
        return [pl.BlockSpec((rows, t, mq), tok), pl.BlockSpec((rows, t, mq), tok),
                pl.BlockSpec((rows, mq, t), rowm), pl.BlockSpec((rows, mq, t), rowm),
                pl.BlockSpec((rows, t, mv), tok),
                pl.BlockSpec((rows, t, V7X_LANES), tok), pl.BlockSpec((rows, ng, t), rowm)]

    return pl.pallas_call(
        kern,
        grid=(b // rows, nc),
        in_specs=stream(fwd, fwd_r) + stream(bwd, bwd_r),
        out_specs=[pl.BlockSpec((rows, t, mv), fwd), pl.BlockSpec((rows, t, mv), bwd)],
        out_shape=[jax.ShapeDtypeStruct((b, l, mv), BF16), jax.ShapeDtypeStruct((b, l, mv), BF16)],
        scratch_shapes=[pltpu.VMEM((chains, dqk, dv), F32), pltpu.VMEM((2, chains, t + dqk, dv), BF16),
                        pltpu.VMEM((chains // 2, N_ROWS, 2 * dqk), F32), pltpu.VMEM((chains, 1, 1), F32)],
        compiler_params=pltpu.CompilerParams(dimension_semantics=("parallel", "arbitrary")),
        name="mlstm_scan",
    )(q, k, qt, kt, v, gc, gr, q, k, qt, kt, v, gc, gr)


def _mout_kernel(hf_ref, hb_ref, x_ref, g_ref, wo_ref, hg_ref, w_ref, out_ref, y_ref, *, heads, dv):
    x = x_ref[0]
    o = jnp.dot(_rmsnorm(x, g_ref[...]).astype(BF16), wo_ref[...], preferred_element_type=F32)
    for h in range(heads):
        sl = slice(h * dv, (h + 1) * dv)
        hs = hf_ref[0, :, sl].astype(F32) + hb_ref[0, :, sl].astype(F32)
        hs = hs * lax.rsqrt(jnp.mean(hs * hs, axis=-1, keepdims=True) + EPS)
        hs = hs * hg_ref[:, sl]
        y_ref[:, sl] = (hs * jax.nn.sigmoid(o[:, sl])).astype(BF16)
    out_ref[0] = x + jnp.dot(y_ref[...], w_ref[...], preferred_element_type=F32)


def _mlstm_out(hf, hb, x, norm_g, w_gate_out, head_g, w_out):
    b, l, d = x.shape
    heads, dv = head_g.shape
    mv = heads * dv
    tm = min(1024, l)
    assert l % tm == 0
    row = lambda bi, i: (bi, i, 0)
    w = w_out.astype(BF16)
    wo = w_gate_out.astype(BF16)
    vmem = (w.size + wo.size) * 2 + 2 * tm * (2 * mv * 2 + 2 * d * 4) + tm * mv * 2 + 8 * tm * d * 4
    return pl.pallas_call(
        functools.partial(_mout_kernel, heads=heads, dv=dv),
        grid=(b, l // tm),
        in_specs=[pl.BlockSpec((1, tm, mv), row), pl.BlockSpec((1, tm, mv), row), pl.BlockSpec((1, tm, d), row),
                  _resident((1, d)), _resident(wo.shape), _resident((1, mv)), _resident(w.shape)],
        out_specs=pl.BlockSpec((1, tm, d), row),
        out_shape=jax.ShapeDtypeStruct((b, l, d), F32),
        scratch_shapes=[pltpu.VMEM((tm, mv), BF16)],
        compiler_params=pltpu.CompilerParams(
            dimension_semantics=("parallel", "parallel"), vmem_limit_bytes=_vmem_limit(vmem)),
        name="mlstm_out",
    )(hf, hb, x, norm_g.reshape(1, d), wo, head_g.reshape(1, mv), w)


def _dft_tables(l1, l2, gd, n2h, n2b):
    l = l1 * l2
    a1 = 2.0 * np.pi * np.outer(np.arange(l1), np.arange(l1)) / l1
    wa = np.stack([np.cos(a1), -np.sin(a1)]) / np.sqrt(l1)
    wa = np.einsum("rkn,ij->krinj", wa, np.eye(n2h)).reshape(l1 * 2 * n2h, l1 * n2h)
    tw = 2.0 * np.pi * np.einsum("k,jhi->jhki", np.arange(l1),
                                 np.arange(l2).reshape(l2 // n2b, n2b // n2h, n2h)) / l
    twr, twi = np.cos(tw)[..., None], -np.sin(tw)[..., None]
    a2 = 2.0 * np.pi * np.outer(np.arange(l2), np.arange(l2)) / l2
    c2, s2 = np.cos(a2), np.sin(a2)
    wc = np.block([[c2, s2], [-s2, c2]]) / np.sqrt(l2)
    ac = 2.0 * np.pi * np.outer(np.arange(gd), np.arange(gd)) / gd
    cs = np.stack([np.cos(ac), np.sin(ac)]) / np.sqrt(gd)
    return (jnp.asarray(wa, BF16), jnp.asarray(twr, F32), jnp.asarray(twi, F32),
            jnp.asarray(wc, BF16), jnp.asarray(cs, BF16))


def _fa_kernel(x_ref, g_ref, wa_ref, twr_ref, twi_ref, out_ref, *, l1, n2h, n2b, d):
    g = g_ref[...]
    wa = wa_ref[...]
    halves = []
    for hh in range(n2b // n2h):
        x = x_ref[0, :, hh * n2h:(hh + 1) * n2h, :].reshape(l1 * n2h, d)
        a = jnp.dot(wa, _rmsnorm(x, g).astype(BF16), preferred_element_type=F32)
        a = a.reshape(l1, 2, n2h, d)
        ar = a[:, 0]
        ai = a[:, 1]
        tr = twr_ref[0, hh]
        ti = twi_ref[0, hh]
        halves.append(jnp.stack([ar * tr - ai * ti, ar * ti + ai * tr], axis=1))
    out_ref[0] = jnp.concatenate(halves, axis=2).astype(BF16)


def _fc_kernel(ba_ref, x_ref, wc_ref, cs_ref, wout_ref, bout_ref, out_ref, z_ref, f_ref, *, l2, k1b, d, gd):
    wc = wc_ref[...]
    for j in range(k1b):
        z = jnp.dot(wc, ba_ref[0, j], preferred_element_type=F32)
        z_ref[0, j * l2:(j + 1) * l2, :] = z[:l2].astype(BF16)
        z_ref[1, j * l2:(j + 1) * l2, :] = z[l2:].astype(BF16)
    for gi in range(d // gd):
        sl = slice(gi * gd, (gi + 1) * gd)
        f = (jnp.dot(z_ref[0, :, sl], cs_ref[0], preferred_element_type=F32)
             + jnp.dot(z_ref[1, :, sl], cs_ref[1], preferred_element_type=F32))
        f_ref[:, sl] = f.astype(BF16)
    y = jnp.dot(f_ref[...], wout_ref[...], preferred_element_type=F32) + bout_ref[...]
    for j in range(k1b):
        out_ref[0, :, j, :] = x_ref[0, :, j, :] + y[j * l2:(j + 1) * l2]


def _fourier_mixer(x, norm_g, w_out, b_out):
    b, l, d = x.shape
    l2 = min(DFT_INNER, l)
    l1 = l // l2
    assert l1 * l2 == l and d % F_GROUPS == 0
    gd = d // F_GROUPS
    n2h = V7X_SUBLANES
    n2b = min(V7X_BF16_ROWS, l2)
    k1b = min(V7X_SUBLANES, l1)
    assert l2 % n2b == 0 and n2b % n2h == 0 and l1 % k1b == 0
    wa, twr, twi, wc, cs = _dft_tables(l1, l2, gd, n2h, n2b)

    tw_spec = pl.BlockSpec((1, n2b // n2h, l1, n2h, 1), lambda bi, j: (j, 0, 0, 0, 0))
    ba = pl.pallas_call(
        functools.partial(_fa_kernel, l1=l1, n2h=n2h, n2b=n2b, d=d),
        grid=(b, l2 // n2b),
        in_specs=[pl.BlockSpec((1, l1, n2b, d), lambda bi, j: (bi, 0, j, 0)),
                  _resident((1, d)), _resident(wa.shape), tw_spec, tw_spec],
        out_specs=pl.BlockSpec((1, l1, 2, n2b, d), lambda bi, j: (bi, 0, 0, j, 0)),
        out_shape=jax.ShapeDtypeStruct((b, l1, 2, l2, d), BF16),
        compiler_params=pltpu.CompilerParams(
            dimension_semantics=("parallel", "parallel"),
            vmem_limit_bytes=_vmem_limit(l1 * n2b * d * (2 * 4 + 2 * 2 * 2 + 6 * 4) + wa.size * 2)),
        name="fourier_seq_a",
    )(x.reshape(b, l1, l2, d), norm_g.reshape(1, d), wa, twr, twi)

    rows = k1b * l2
    out = pl.pallas_call(
        functools.partial(_fc_kernel, l2=l2, k1b=k1b, d=d, gd=gd),
        grid=(b, l1 // k1b),
        in_specs=[pl.BlockSpec((1, k1b, 2 * l2, d), lambda bi, j: (bi, j, 0, 0)),
                  pl.BlockSpec((1, l2, k1b, d), lambda bi, j: (bi, 0, j, 0)),
                  _resident(wc.shape), _resident(cs.shape), _resident((d, d)), _resident((1, d))],
        out_specs=pl.BlockSpec((1, l2, k1b, d), lambda bi, j: (bi, 0, j, 0)),
        out_shape=jax.ShapeDtypeStruct((b, l2, l1, d), F32),
        scratch_shapes=[pltpu.VMEM((2, rows, d), BF16), pltpu.VMEM((rows, d), BF16)],
        compiler_params=pltpu.CompilerParams(
            dimension_semantics=("parallel", "parallel"),
            vmem_limit_bytes=_vmem_limit(rows * d * (2 * 2 * 2 + 4 * 4 + 3 * 2 + 4 * 4) + 4 * d * d)),
        name="fourier_seq_c",
    )(ba.reshape(b, l1, 2 * l2, d), x.reshape(b, l2, l1, d), wc, cs, w_out.astype(BF16), b_out.reshape(1, d))
    return out.reshape(b, l, d)


def _mlstm_mixer(x, norm_g, w_in, b_gate, head_g, w_out):
    heads, dv = head_g.shape
    mv = heads * dv
    q, k, qt, kt, v, gc, gr = _mlstm_proj(x, norm_g, w_in, b_gate, heads=heads, mv=mv)
    hf, hb = _mlstm_scan(q, k, qt, kt, v, gc, gr, heads=heads)
    o_lo = 2 * q.shape[-1] + mv
    return _mlstm_out(hf, hb, x, norm_g, w_in[:, o_lo:o_lo + mv], head_g, w_out)


def _trunk(x, mix_norm_g, m_w_in, m_b_gate, m_head_g, m_w_out, f_w_out, f_b_out,
           ffn_norm_g, ffn_w_up, ffn_conv_w, ffn_conv_b, ffn_w_down, final_norm_g):
    depth = mix_norm_g.shape[0]
    n_mixers = 2
    for i in range(depth):
        j = i // n_mixers
        if i % n_mixers == 0:
            x = _mlstm_mixer(x, mix_norm_g[i], m_w_in[j], m_b_gate[j], m_head_g[j], m_w_out[j])
        else:
            x = _fourier_mixer(x, mix_norm_g[i], f_w_out[j], f_b_out[j])
        x = _conv_ffn(x, ffn_norm_g[i], ffn_w_up[i], ffn_conv_w[i], ffn_conv_b[i], ffn_w_down[i],
                      final_norm_g, final_norm=(i == depth - 1))
    return x


def kernel(x_prompt, x_sample, mix_norm_g, m_w_in, m_b_gate, m_head_g, m_w_out, f_w_out, f_b_out,
           ffn_norm_g, ffn_w_up, ffn_conv_w, ffn_conv_b, ffn_w_down, final_norm_g):
    params = (mix_norm_g, m_w_in, m_b_gate, m_head_g, m_w_out, f_w_out, f_b_out,
              ffn_norm_g, ffn_w_up, ffn_conv_w, ffn_conv_b, ffn_w_down, final_norm_g)
    return (_trunk(x_prompt, *params), _trunk(x_sample, *params))
```

```python
import functools

import numpy as np
import jax
import jax.numpy as jnp
from jax import lax
from jax.experimental import pallas as pl
from jax.experimental.pallas import tpu as pltpu

F32 = jnp.float32
BF16 = jnp.bfloat16

EPS = 1e-6
F_GROUPS = 4
N_GATE_KINDS = 4

V7X_VMEM_BYTES = 64 * 1024 * 1024
V7X_LANES = 128
V7X_SUBLANES = 8
V7X_BF16_ROWS = 16
V7X_MXU_DIM = 256

HALO = V7X_SUBLANES
SCAN_CHUNK = 128
N_ROWS = V7X_BF16_ROWS
SCAN_ROWS = 2
DFT_INNER = 128


def _vmem_limit(nbytes):
    return int(min(max(nbytes, 32 * 1024 * 1024), V7X_VMEM_BYTES - 8 * 1024 * 1024))


def _rmsnorm(x, g):
    y = x * lax.rsqrt(jnp.mean(x * x, axis=-1, keepdims=True) + EPS)
    return y * g


def _resident(shape):
    nd = len(shape)
    return pl.BlockSpec(shape, lambda *_: (0,) * nd, pipeline_mode=pl.Buffered(1))


def _ffn_kernel(xm_ref, xp_ref, xn_ref, g_ref, wup_ref, cw_ref, cb_ref, wdn_ref, fg_ref,
                o_ref, h_ref, act_ref, *, tm, n_chunks, fc, conv_w, final_norm):
    i = pl.program_id(1)
    last = pl.num_programs(1) - 1
    g = g_ref[...]
    x = xm_ref[0]
    hp = jnp.where(i > 0, _rmsnorm(xp_ref[0], g), 0.0)
    hn = jnp.where(i < last, _rmsnorm(xn_ref[0], g), 0.0)
    h_ref[0:HALO, :] = hp.astype(BF16)
    h_ref[HALO:HALO + tm, :] = _rmsnorm(x, g).astype(BF16)
    h_ref[HALO + tm:, :] = hn.astype(BF16)
    hb = h_ref[...]
    pad = conv_w // 2
    d_ff = n_chunks * fc

    def conv_cols(lo):
        u = jnp.dot(hb, wup_ref[:, lo:lo + fc], preferred_element_type=F32)
        y = cb_ref[:, lo:lo + fc]
        for j in [pad] + [j for j in range(conv_w) if j != pad]:
            off = HALO - pad + j
            y = y + u[off:off + tm] * cw_ref[j:j + 1, lo:lo + fc]
        return y

    for c in range(n_chunks):
        a = conv_cols(c * fc)
        val = conv_cols(d_ff + c * fc)
        act_ref[:, c * fc:(c + 1) * fc] = (a * jax.nn.sigmoid(a) * val).astype(BF16)
    r = x + jnp.dot(act_ref[...], wdn_ref[...], preferred_element_type=F32)
    if final_norm:
        r = _rmsnorm(r, fg_ref[...])
    o_ref[0] = r


def _conv_ffn(x, norm_g, w_up, conv_w, conv_b, w_down, final_g, *, final_norm):
    b, l, d = x.shape
    d_ff = w_down.shape[0]
    cw = conv_w.shape[0]
    fc = V7X_MXU_DIM
    n_chunks = d_ff // fc
    assert n_chunks * fc == d_ff and cw // 2 <= HALO
    tm = min(1024, l)
    assert l % tm == 0 and tm % HALO == 0
    nt = l // tm
    rb = tm // HALO
    assert (2 * d_ff) % V7X_LANES == 0 and fc % V7X_LANES == 0
    wup = w_up.astype(BF16)
    cwr = conv_w
    cbr = conv_b.reshape(1, 2 * d_ff)
    wdn = w_down.astype(BF16)
    kern = functools.partial(_ffn_kernel, tm=tm, n_chunks=n_chunks, fc=fc, conv_w=cw, final_norm=final_norm)
    vmem = (wup.size * 2 + wdn.size * 2 + 4 * tm * d * 4 + (tm + 2 * HALO) * d * 2 + tm * d_ff * 2
            + 8 * (tm + 2 * HALO) * fc * 4 + 3 * tm * d * 4)
    return pl.pallas_call(
        kern,
        grid=(b, nt),
        in_specs=[
            pl.BlockSpec((1, tm, d), lambda bi, i: (bi, i, 0)),
            pl.BlockSpec((1, HALO, d), lambda bi, i: (bi, jnp.maximum(i * rb - 1, 0), 0)),
            pl.BlockSpec((1, HALO, d), lambda bi, i: (bi, jnp.minimum((i + 1) * rb, l // HALO - 1), 0)),
            _resident((1, d)),
            _resident(wup.shape),
            _resident(cwr.shape),
            _resident(cbr.shape),
            _resident(wdn.shape),
            _resident((1, d)),
        ],
        out_specs=pl.BlockSpec((1, tm, d), lambda bi, i: (bi, i, 0)),
        out_shape=jax.ShapeDtypeStruct((b, l, d), F32),
        scratch_shapes=[pltpu.VMEM((tm + 2 * HALO, d), BF16), pltpu.VMEM((tm, d_ff), BF16)],
        compiler_params=pltpu.CompilerParams(
            dimension_semantics=("parallel", "arbitrary"), vmem_limit_bytes=_vmem_limit(vmem)),
        name="conv_ffn",
    )(x, x, x, norm_g.reshape(1, d), wup, cwr, cbr, wdn, final_g.reshape(1, d))


def _split3(x):
    hi = x.astype(BF16)
    r = x - hi.astype(F32)
    mid = r.astype(BF16)
    lo = (r - mid.astype(F32)).astype(BF16)
    return hi, mid, lo


def _proj_kernel(x_ref, g_ref, w_ref, wg_ref, bg_ref, pre_ref, suf_ref,
                 q_ref, k_ref, qt_ref, kt_ref, v_ref, gc_ref, gr_ref, *, mq, mv, heads, k_scale):
    h = _rmsnorm(x_ref[0], g_ref[...]).astype(BF16)
    q = jnp.dot(h, w_ref[:, 0:mq], preferred_element_type=F32)
    k = jnp.dot(h, w_ref[:, mq:2 * mq], preferred_element_type=F32) * k_scale
    q_ref[0] = q.astype(BF16)
    k_ref[0] = k.astype(BF16)
    qt_ref[0] = q.T.astype(BF16)
    kt_ref[0] = k.T.astype(BF16)
    v_ref[0] = jnp.dot(h, w_ref[:, 2 * mq:2 * mq + mv], preferred_element_type=F32).astype(BF16)
    gt = jnp.dot(h, wg_ref[...], preferred_element_type=F32) + bg_ref[...]
    lane = lax.broadcasted_iota(jnp.int32, gt.shape, 1)
    kind = lane // heads
    log_sig = jnp.minimum(gt, 0.0) - jnp.log1p(jnp.exp(-jnp.abs(gt)))
    parts = _split3(log_sig)
    t = pre_ref.shape[0]
    pre, suf = [], []
    for c in range(gt.shape[0] // t):
        rows = slice(c * t, (c + 1) * t)
        pre.append(sum(jnp.dot(pre_ref[...], p[rows], preferred_element_type=F32) for p in parts))
        suf.append(sum(jnp.dot(suf_ref[...], p[rows], preferred_element_type=F32) for p in parts))
    gt = jnp.where(kind == 1, jnp.concatenate(pre, axis=0), jnp.where(kind == 3, jnp.concatenate(suf, axis=0), gt))
    gc_ref[0] = gt
    gr_ref[0] = gt.T[0:gr_ref.shape[1], :]


def _mlstm_proj(x, norm_g, w_in, b_gate, *, heads, mv):
    b, l, d = x.shape
    ng = N_GATE_KINDS * heads
    mq = (w_in.shape[1] - 2 * mv - ng) // 2
    dqk = mq // heads
    tm = min(1024, l)
    assert l % tm == 0 and ng <= V7X_LANES
    w_main = w_in[:, :2 * mq + mv].astype(BF16)
    w_gate = jnp.pad(w_in[:, 2 * mq + 2 * mv:], ((0, 0), (0, V7X_LANES - ng))).astype(BF16)
    b_pad = jnp.pad(b_gate, (0, V7X_LANES - ng)).reshape(1, V7X_LANES)
    t = min(SCAN_CHUNK, l)
    assert tm % t == 0
    pos = np.arange(t)
    pre_m = jnp.asarray(pos[None, :] <= pos[:, None], BF16)
    suf_m = jnp.asarray(pos[None, :] >= pos[:, None], BF16)
    kern = functools.partial(_proj_kernel, mq=mq, mv=mv, heads=heads, k_scale=float(dqk) ** -0.5)
    row = lambda bi, i: (bi, i, 0)
    col = lambda bi, i: (bi, 0, i)
    vmem = (w_main.size * 2 + w_gate.size * 2 + 2 * tm * d * 4 + 2 * tm * (4 * mq + mv) * 2 + 8 * tm * mv * 4)
    return pl.pallas_call(
        kern,
        grid=(b, l // tm),
        in_specs=[
            pl.BlockSpec((1, tm, d), row),
            _resident((1, d)),
            _resident(w_main.shape),
            _resident(w_gate.shape),
            _resident((1, V7X_LANES)),
            _resident((t, t)),
            _resident((t, t)),
        ],
        out_specs=[
            pl.BlockSpec((1, tm, mq), row),
            pl.BlockSpec((1, tm, mq), row),
            pl.BlockSpec((1, mq, tm), col),
            pl.BlockSpec((1, mq, tm), col),
            pl.BlockSpec((1, tm, mv), row),
            pl.BlockSpec((1, tm, V7X_LANES), row),
            pl.BlockSpec((1, ng, tm), col),
        ],
        out_shape=[
            jax.ShapeDtypeStruct((b, l, mq), BF16),
            jax.ShapeDtypeStruct((b, l, mq), BF16),
            jax.ShapeDtypeStruct((b, mq, l), BF16),
            jax.ShapeDtypeStruct((b, mq, l), BF16),
            jax.ShapeDtypeStruct((b, l, mv), BF16),
            jax.ShapeDtypeStruct((b, l, V7X_LANES), F32),
            jax.ShapeDtypeStruct((b, ng, l), F32),
        ],
        compiler_params=pltpu.CompilerParams(
            dimension_semantics=("parallel", "parallel"), vmem_limit_bytes=_vmem_limit(vmem)),
        name="mlstm_proj",
    )(x, norm_g.reshape(1, d), w_main, w_gate, b_pad, pre_m, suf_m)


def _scan_kernel(qf_ref, kf_ref, qtf_ref, ktf_ref, vf_ref, gcf_ref, grf_ref,
                 qb_ref, kb_ref, qtb_ref, ktb_ref, vb_ref, gcb_ref, grb_ref,
                 hf_ref, hb_ref, c_ref, rhs_ref, n_ref, m_ref, *, t, heads, dqk, dv):
    step = pl.program_id(1)
    par = step % 2

    @pl.when(step == 0)
    def _():
        c_ref[...] = jnp.zeros_like(c_ref)
        rhs_ref[...] = jnp.zeros_like(rhs_ref)
        n_ref[...] = jnp.zeros_like(n_ref)
        m_ref[...] = jnp.zeros_like(m_ref)

    src_i = lax.broadcasted_iota(jnp.int32, (t, t), 0)
    tgt_i = lax.broadcasted_iota(jnp.int32, (t, t), 1)
    first_head = lax.broadcasted_iota(jnp.int32, (t, 2 * dqk), 1) < dqk
    n_row = lax.broadcasted_iota(jnp.int32, (N_ROWS, 2 * t), 0)
    n_lane_first = lax.broadcasted_iota(jnp.int32, (N_ROWS, 2 * t), 1) < t
    own_block = ((n_row == 0) & n_lane_first) | ((n_row == 1) & jnp.logical_not(n_lane_first))
    n_row_q = lax.broadcasted_iota(jnp.int32, (N_ROWS, 2 * dqk), 0)
    fwd_refs = (qf_ref, kf_ref, qtf_ref, ktf_ref, vf_ref, gcf_ref, grf_ref, hf_ref)
    bwd_refs = (qb_ref, kb_ref, qtb_ref, ktb_ref, vb_ref, gcb_ref, grb_ref, hb_ref)
    streams = []
    for bb in range(hf_ref.shape[0]):
        streams.append(tuple(r.at[bb:bb + 1] for r in fwd_refs) + (src_i <= tgt_i, 0, t - 1))
        streams.append(tuple(r.at[bb:bb + 1] for r in bwd_refs) + (src_i >= tgt_i, 2 * heads, 0))
    dn_nt = (((1,), (1,)), ((), ()))
    dn_tn = (((0,), (0,)), ((), ()))
    half = heads // 2

    def block_diag(a2):
        zero = jnp.zeros_like(a2)
        return jnp.concatenate([jnp.where(first_head, a2, zero), jnp.where(first_head, zero, a2)], axis=0)

    every = [(di, h, di * heads + h) for di in range(len(streams)) for h in range(heads)]
    pairs = [(di, p, di * half + p) for di in range(len(streams)) for p in range(half)]
    st = {}
    for di, h, idx in every:
        gc_ref, gr_ref, base, end = streams[di][5], streams[di][6], streams[di][9], streams[di][10]
        ig_r = gr_ref[0, base + h:base + h + 1, :]
        b_r = gr_ref[0, base + heads + h:base + heads + h + 1, :]
        total = b_r[:, end:end + 1]
        m_prev = m_ref[idx]
        st[idx] = dict(b_r=b_r, inter=b_r + m_prev, gk=total - b_r + ig_r, carry=total + m_prev,
                       src_c=gc_ref[0, :, base + h:base + h + 1] - gc_ref[0, :, base + heads + h:base + heads + h + 1])
    for di, h, idx in every:
        e = st[idx]
        m_new = jnp.maximum(e["carry"], jnp.max(e["gk"], axis=-1, keepdims=True))
        m_ref[idx] = m_new
        e["wk"] = jnp.exp(e["gk"] - m_new)
        e["decay"] = jnp.exp(e["carry"] - m_new)

    sp = {}
    for di, p, pi in pairs:
        sl = slice(2 * p * dqk, (2 * p + 2) * dqk)
        sp[pi] = dict(k2=streams[di][1][0, :, sl], n_prev=n_ref[pi])
        lhs = jnp.concatenate([sp[pi]["k2"], sp[pi]["n_prev"].astype(BF16)], axis=0)
        s2 = lax.dot_general(lhs, block_diag(streams[di][0][0, :, sl]), dn_nt, preferred_element_type=F32)
        for j in range(2):
            e = st[di * heads + 2 * p + j]
            e["qk"] = s2[:t, j * t:(j + 1) * t]
            e["qn"] = s2[t + j:t + j + 1, j * t:(j + 1) * t]

    for di, h, idx in every:
        e = st[idx]
        e["v"] = streams[di][4][0, :, h * dv:(h + 1) * dv]
        e["kwt"] = (streams[di][3][0, h * dqk:(h + 1) * dqk, :].astype(F32) * e["wk"]).astype(BF16)
    for di, h, idx in every:
        e = st[idx]
        e["upd"] = jnp.dot(e["kwt"], e["v"], preferred_element_type=F32)
        rhs_ref[par, idx, :t, :] = e["v"]
    for di, p, pi in pairs:
        ea, eb = st[di * heads + 2 * p], st[di * heads + 2 * p + 1]
        wk2 = jnp.concatenate([ea["wk"], eb["wk"]], axis=1)
        w_rows = jnp.where(own_block, wk2, 0.0).astype(BF16)
        dec = jnp.where(n_row_q == 0, ea["decay"], eb["decay"])
        n_ref[pi] = dec * sp[pi]["n_prev"] + jnp.dot(w_rows, block_diag(sp[pi]["k2"]), preferred_element_type=F32)
    for di, h, idx in every:
        e = st[idx]
        c_new = e["decay"] * c_ref[idx] + e["upd"]
        c_ref[idx] = c_new
        rhs_ref[1 - par, idx, t:, :] = c_new.astype(BF16)

    for di, h, idx in every:
        e = st[idx]
        e["dlog"] = jnp.where(streams[di][8], e["src_c"] + e["b_r"], -jnp.inf)
    for di, h, idx in every:
        e = st[idx]
        e["m_t"] = jnp.maximum(e["inter"], jnp.max(e["dlog"], axis=0, keepdims=True))
    for di, h, idx in every:
        e = st[idx]
        e["s"] = e["qk"] * jnp.exp(e["dlog"] - e["m_t"])
        e["sc"] = jnp.exp(e["inter"] - e["m_t"])
    for di, h, idx in every:
        e = st[idx]
        den = jnp.sum(e["s"], axis=0, keepdims=True) + e["sc"] * e["qn"]
        e["r"] = 1.0 / jnp.maximum(jnp.abs(den), jnp.exp(-e["m_t"]))
    for di, h, idx in every:
        e = st[idx]
        qt32 = streams[di][2][0, h * dqk:(h + 1) * dqk, :].astype(F32)
        e["lhs_t"] = jnp.concatenate(
            [(e["s"] * e["r"]).astype(BF16), (qt32 * (e["sc"] * e["r"])).astype(BF16)], axis=0)
    for di, h, idx in every:
        streams[di][7][0, :, h * dv:(h + 1) * dv] = lax.dot_general(
            st[idx]["lhs_t"], rhs_ref[par, idx], dn_tn, preferred_element_type=F32).astype(streams[di][7].dtype)


def _mlstm_scan(q, k, qt, kt, v, gc, gr, *, heads):
    b, l, mq = q.shape
    mv = v.shape[-1]
    dqk, dv = mq // heads, mv // heads
    t = min(SCAN_CHUNK, l)
    assert l % t == 0 and heads % 2 == 0
    nc = l // t
    ng = gr.shape[1]
    kern = functools.partial(_scan_kernel, t=t, heads=heads, dqk=dqk, dv=dv)
    fwd = lambda bi, c: (bi, c, 0)
    bwd = lambda bi, c: (bi, nc - 1 - c, 0)
    fwd_r = lambda bi, c: (bi, 0, c)
    bwd_r = lambda bi, c: (bi, 0, nc - 1 - c)

    rows = SCAN_ROWS if b % SCAN_ROWS == 0 else 1
    chains = 2 * rows * heads

    def stream(tok, rowm):
        return [pl.BlockSpec((rows, t, mq), tok), pl.BlockSpec((rows, t, mq), tok),
                pl.BlockSpec((rows, mq, t), rowm), pl.BlockSpec((rows, mq, t), rowm),
                pl.BlockSpec((rows, t, mv), tok),
                pl.BlockSpec((rows, t, V7X_LANES), tok), pl.BlockSpec((rows, ng, t), rowm)]

    return pl.pallas_call(
        kern,
        grid=(b // rows, nc),
        in_specs=stream(fwd, fwd_r) + stream(bwd, bwd_r),
        out_specs=[pl.BlockSpec((rows, t, mv), fwd), pl.BlockSpec((rows, t, mv), bwd)],
        out_shape=[jax.ShapeDtypeStruct((b, l, mv), BF16), jax.ShapeDtypeStruct((b, l, mv), BF16)],
        scratch_shapes=[pltpu.VMEM((chains, dqk, dv), F32), pltpu.VMEM((2, chains, t + dqk, dv), BF16),
                        pltpu.VMEM((chains // 2, N_ROWS, 2 * dqk), F32), pltpu.VMEM((chains, 1, 1), F32)],
        compiler_params=pltpu.CompilerParams(dimension_semantics=("parallel", "arbitrary")),
        name="mlstm_scan",
    )(q, k, qt, kt, v, gc, gr, q, k, qt, kt, v, gc, gr)


def _mout_kernel(hf_ref, hb_ref, x_ref, g_ref, wo_ref, hg_ref, w_ref, out_ref, y_ref, *, heads, dv):
    x = x_ref[0]
    o = jnp.dot(_rmsnorm(x, g_ref[...]).astype(BF16), wo_ref[...], preferred_element_type=F32)
    for h in range(heads):
        sl = slice(h * dv, (h + 1) * dv)
        hs = hf_ref[0, :, sl].astype(F32) + hb_ref[0, :, sl].astype(F32)
        hs = hs * lax.rsqrt(jnp.mean(hs * hs, axis=-1, keepdims=True) + EPS)
        hs = hs * hg_ref[:, sl]
        y_ref[:, sl] = (hs * jax.nn.sigmoid(o[:, sl])).astype(BF16)
    out_ref[0] = x + jnp.dot(y_ref[...], w_ref[...], preferred_element_type=F32)


def _mlstm_out(hf, hb, x, norm_g, w_gate_out, head_g, w_out):
    b, l, d = x.shape
    heads, dv = head_g.shape
    mv = heads * dv
    tm = min(1024, l)
    assert l % tm == 0
    row = lambda bi, i: (bi, i, 0)
    w = w_out.astype(BF16)
    wo = w_gate_out.astype(BF16)
    vmem = (w.size + wo.size) * 2 + 2 * tm * (2 * mv * 2 + 2 * d * 4) + tm * mv * 2 + 8 * tm * d * 4
    return pl.pallas_call(
        functools.partial(_mout_kernel, heads=heads, dv=dv),
        grid=(b, l // tm),
        in_specs=[pl.BlockSpec((1, tm, mv), row), pl.BlockSpec((1, tm, mv), row), pl.BlockSpec((1, tm, d), row),
                  _resident((1, d)), _resident(wo.shape), _resident((1, mv)), _resident(w.shape)],
        out_specs=pl.BlockSpec((1, tm, d), row),
        out_shape=jax.ShapeDtypeStruct((b, l, d), F32),
        scratch_shapes=[pltpu.VMEM((tm, mv), BF16)],
        compiler_params=pltpu.CompilerParams(
            dimension_semantics=("parallel", "parallel"), vmem_limit_bytes=_vmem_limit(vmem)),
        name="mlstm_out",
    )(hf, hb, x, norm_g.reshape(1, d), wo, head_g.reshape(1, mv), w)


def _dft_tables(l1, l2, gd, n2h, n2b):
    l = l1 * l2
    a1 = 2.0 * np.pi * np.outer(np.arange(l1), np.arange(l1)) / l1
    wa = np.stack([np.cos(a1), -np.sin(a1)]) / np.sqrt(l1)
    wa = np.einsum("rkn,ij->krinj", wa, np.eye(n2h)).reshape(l1 * 2 * n2h, l1 * n2h)
    tw = 2.0 * np.pi * np.einsum("k,jhi->jhki", np.arange(l1),
                                 np.arange(l2).reshape(l2 // n2b, n2b // n2h, n2h)) / l
    twr, twi = np.cos(tw)[..., None], -np.sin(tw)[..., None]
    a2 = 2.0 * np.pi * np.outer(np.arange(l2), np.arange(l2)) / l2
    c2, s2 = np.cos(a2), np.sin(a2)
    wc = np.block([[c2, s2], [-s2, c2]]) / np.sqrt(l2)
    ac = 2.0 * np.pi * np.outer(np.arange(gd), np.arange(gd)) / gd
    cs = np.stack([np.cos(ac), np.sin(ac)]) / np.sqrt(gd)
    return (jnp.asarray(wa, BF16), jnp.asarray(twr, F32), jnp.asarray(twi, F32),
            jnp.asarray(wc, BF16), jnp.asarray(cs, BF16))


def _fa_kernel(x_ref, g_ref, wa_ref, twr_ref, twi_ref, out_ref, *, l1, n2h, n2b, d):
    g = g_ref[...]
    wa = wa_ref[...]
    halves = []
    for hh in range(n2b // n2h):
        x = x_ref[0, :, hh * n2h:(hh + 1) * n2h, :].reshape(l1 * n2h, d)
        a = jnp.dot(wa, _rmsnorm(x, g).astype(BF16), preferred_element_type=F32)
        a = a.reshape(l1, 2, n2h, d)
        ar = a[:, 0]
        ai = a[:, 1]
        tr = twr_ref[0, hh]
        ti = twi_ref[0, hh]
        halves.append(jnp.stack([ar * tr - ai * ti, ar * ti + ai * tr], axis=1))
    out_ref[0] = jnp.concatenate(halves, axis=2).astype(BF16)


def _fc_kernel(ba_ref, x_ref, wc_ref, cs_ref, wout_ref, bout_ref, out_ref, z_ref, f_ref, *, l2, k1b, d, gd):
    wc = wc_ref[...]
    for j in range(k1b):
        z = jnp.dot(wc, ba_ref[0, j], preferred_element_type=F32)
        z_ref[0, j * l2:(j + 1) * l2, :] = z[:l2].astype(BF16)
        z_ref[1, j * l2:(j + 1) * l2, :] = z[l2:].astype(BF16)
    for gi in range(d // gd):
        sl = slice(gi * gd, (gi + 1) * gd)
        f = (jnp.dot(z_ref[0, :, sl], cs_ref[0], preferred_element_type=F32)
             + jnp.dot(z_ref[1, :, sl], cs_ref[1], preferred_element_type=F32))
        f_ref[:, sl] = f.astype(BF16)
    y = jnp.dot(f_ref[...], wout_ref[...], preferred_element_type=F32) + bout_ref[...]
    out_ref[0] = x_ref[0] + pltpu.einshape("jkd->kjd", y.reshape(k1b, l2, d))


def _fourier_mixer(x, norm_g, w_out, b_out):
    b, l, d = x.shape
    l2 = min(DFT_INNER, l)
    l1 = l // l2
    assert l1 * l2 == l and d % F_GROUPS == 0
    gd = d // F_GROUPS
    n2h = V7X_SUBLANES
    n2b = min(V7X_BF16_ROWS, l2)
    k1b = min(V7X_SUBLANES, l1)
    assert l2 % n2b == 0 and n2b % n2h == 0 and l1 % k1b == 0
    wa, twr, twi, wc, cs = _dft_tables(l1, l2, gd, n2h, n2b)

    tw_spec = pl.BlockSpec((1, n2b // n2h, l1, n2h, 1), lambda bi, j: (j, 0, 0, 0, 0))
    ba = pl.pallas_call(
        functools.partial(_fa_kernel, l1=l1, n2h=n2h, n2b=n2b, d=d),
        grid=(b, l2 // n2b),
        in_specs=[pl.BlockSpec((1, l1, n2b, d), lambda bi, j: (bi, 0, j, 0)),
                  _resident((1, d)), _resident(wa.shape), tw_spec, tw_spec],
        out_specs=pl.BlockSpec((1, l1, 2, n2b, d), lambda bi, j: (bi, 0, 0, j, 0)),
        out_shape=jax.ShapeDtypeStruct((b, l1, 2, l2, d), BF16),
        compiler_params=pltpu.CompilerParams(
            dimension_semantics=("parallel", "parallel"),
            vmem_limit_bytes=_vmem_limit(l1 * n2b * d * (2 * 4 + 2 * 2 * 2 + 6 * 4) + wa.size * 2)),
        name="fourier_seq_a",
    )(x.reshape(b, l1, l2, d), norm_g.reshape(1, d), wa, twr, twi)

    rows = k1b * l2
    out = pl.pallas_call(
        functools.partial(_fc_kernel, l2=l2, k1b=k1b, d=d, gd=gd),
        grid=(b, l1 // k1b),
        in_specs=[pl.BlockSpec((1, k1b, 2 * l2, d), lambda bi, j: (bi, j, 0, 0)),
                  pl.BlockSpec((1, l2, k1b, d), lambda bi, j: (bi, 0, j, 0)),
                  _resident(wc.shape), _resident(cs.shape), _resident((d, d)), _resident((1, d))],
        out_specs=pl.BlockSpec((1, l2, k1b, d), lambda bi, j: (bi, 0, j, 0)),
        out_shape=jax.ShapeDtypeStruct((b, l2, l1, d), F32),
        scratch_shapes=[pltpu.VMEM((2, rows, d), BF16), pltpu.VMEM((rows, d), BF16)],
        compiler_params=pltpu.CompilerParams(
            dimension_semantics=("parallel", "parallel"),
            vmem_limit_bytes=_vmem_limit(rows * d * (2 * 2 * 2 + 4 * 4 + 3 * 2 + 4 * 4) + 4 * d * d)),
        name="fourier_seq_c",
    )(ba.reshape(b, l1, 2 * l2, d), x.reshape(b, l2, l1, d), wc, cs, w_out.astype(BF16), b_out.reshape(1, d))
    return out.reshape(b, l, d)


def _mlstm_mixer(x, norm_g, w_in, b_gate, head_g, w_out):
    heads, dv = head_g.shape
    mv = heads * dv
    q, k, qt, kt, v, gc, gr = _mlstm_proj(x, norm_g, w_in, b_gate, heads=heads, mv=mv)
    hf, hb = _mlstm_scan(q, k, qt, kt, v, gc, gr, heads=heads)
    o_lo = 2 * q.shape[-1] + mv
    return _mlstm_out(hf, hb, x, norm_g, w_in[:, o_lo:o_lo + mv], head_g, w_out)


def _trunk(x, mix_norm_g, m_w_in, m_b_gate, m_head_g, m_w_out, f_w_out, f_b_out,
           ffn_norm_g, ffn_w_up, ffn_conv_w, ffn_conv_b, ffn_w_down, final_norm_g):
    depth = mix_norm_g.shape[0]
    n_mixers = 2
    for i in range(depth):
        j = i // n_mixers
        if i % n_mixers == 0:
            x = _mlstm_mixer(x, mix_norm_g[i], m_w_in[j], m_b_gate[j], m_head_g[j], m_w_out[j])
        else:
            x = _fourier_mixer(x, mix_norm_g[i], f_w_out[j], f_b_out[j])
        x = _conv_ffn(x, ffn_norm_g[i], ffn_w_up[i], ffn_conv_w[i], ffn_conv_b[i], ffn_w_down[i],
                      final_norm_g, final_norm=(i == depth - 1))
    return x


def kernel(x_prompt, x_sample, mix_norm_g, m_w_in, m_b_gate, m_head_g, m_w_out, f_w_out, f_b_out,
           ffn_norm_g, ffn_w_up, ffn_conv_w, ffn_conv_b, ffn_w_down, final_norm_g):
    params = (mix_norm_g, m_w_in, m_b_gate, m_head_g, m_w_out, f_w_out, f_b_out,
              ffn_norm_g, ffn_w_up, ffn_conv_w, ffn_conv_b, ffn_w_down, final_norm_g)
    return (_trunk(x_prompt, *params), _trunk(x_sample, *params))
```

```python
import functools

import numpy as np
import jax
import jax.numpy as jnp
from jax import lax
from jax.experimental import pallas as pl
from jax.experimental.pallas import tpu as pltpu

F32 = jnp.float32
BF16 = jnp.bfloat16

EPS = 1e-6
F_GROUPS = 4
N_GATE_KINDS = 4

V7X_VMEM_BYTES = 64 * 1024 * 1024
V7X_LANES = 128
V7X_SUBLANES = 8
V7X_BF16_ROWS = 16
V7X_MXU_DIM = 256

HALO = V7X_SUBLANES
SCAN_CHUNK = 128
N_ROWS = V7X_BF16_ROWS
SCAN_ROWS = 4
DFT_INNER = 128


def _vmem_limit(nbytes):
    return int(min(max(nbytes, 32 * 1024 * 1024), V7X_VMEM_BYTES - 8 * 1024 * 1024))


def _rmsnorm(x, g):
    y = x * lax.rsqrt(jnp.mean(x * x, axis=-1, keepdims=True) + EPS)
    return y * g


def _resident(shape):
    nd = len(shape)
    return pl.BlockSpec(shape, lambda *_: (0,) * nd, pipeline_mode=pl.Buffered(1))


def _ffn_kernel(xm_ref, xp_ref, xn_ref, g_ref, wup_ref, cw_ref, cb_ref, wdn_ref, fg_ref,
                o_ref, h_ref, act_ref, *, tm, n_chunks, fc, conv_w, final_norm):
    i = pl.program_id(1)
    last = pl.num_programs(1) - 1
    g = g_ref[...]
    x = xm_ref[0]
    hp = jnp.where(i > 0, _rmsnorm(xp_ref[0], g), 0.0)
    hn = jnp.where(i < last, _rmsnorm(xn_ref[0], g), 0.0)
    h_ref[0:HALO, :] = hp.astype(BF16)
    h_ref[HALO:HALO + tm, :] = _rmsnorm(x, g).astype(BF16)
    h_ref[HALO + tm:, :] = hn.astype(BF16)
    hb = h_ref[...]
    pad = conv_w // 2
    d_ff = n_chunks * fc

    def conv_cols(lo):
        u = jnp.dot(hb, wup_ref[:, lo:lo + fc], preferred_element_type=F32)
        y = cb_ref[:, lo:lo + fc]
        for j in [pad] + [j for j in range(conv_w) if j != pad]:
            off = HALO - pad + j
            y = y + u[off:off + tm] * cw_ref[j:j + 1, lo:lo + fc]
        return y

    for c in range(n_chunks):
        a = conv_cols(c * fc)
        val = conv_cols(d_ff + c * fc)
        act_ref[:, c * fc:(c + 1) * fc] = (a * jax.nn.sigmoid(a) * val).astype(BF16)
    r = x + jnp.dot(act_ref[...], wdn_ref[...], preferred_element_type=F32)
    if final_norm:
        r = _rmsnorm(r, fg_ref[...])
    o_ref[0] = r


def _conv_ffn(x, norm_g, w_up, conv_w, conv_b, w_down, final_g, *, final_norm):
    b, l, d = x.shape
    d_ff = w_down.shape[0]
    cw = conv_w.shape[0]
    fc = V7X_MXU_DIM
    n_chunks = d_ff // fc
    assert n_chunks * fc == d_ff and cw // 2 <= HALO
    tm = min(1024, l)
    assert l % tm == 0 and tm % HALO == 0
    nt = l // tm
    rb = tm // HALO
    assert (2 * d_ff) % V7X_LANES == 0 and fc % V7X_LANES == 0
    wup = w_up.astype(BF16)
    cwr = conv_w
    cbr = conv_b.reshape(1, 2 * d_ff)
    wdn = w_down.astype(BF16)
    kern = functools.partial(_ffn_kernel, tm=tm, n_chunks=n_chunks, fc=fc, conv_w=cw, final_norm=final_norm)
    vmem = (wup.size * 2 + wdn.size * 2 + 4 * tm * d * 4 + (tm + 2 * HALO) * d * 2 + tm * d_ff * 2
            + 8 * (tm + 2 * HALO) * fc * 4 + 3 * tm * d * 4)
    return pl.pallas_call(
        kern,
        grid=(b, nt),
        in_specs=[
            pl.BlockSpec((1, tm, d), lambda bi, i: (bi, i, 0)),
            pl.BlockSpec((1, HALO, d), lambda bi, i: (bi, jnp.maximum(i * rb - 1, 0), 0)),
            pl.BlockSpec((1, HALO, d), lambda bi, i: (bi, jnp.minimum((i + 1) * rb, l // HALO - 1), 0)),
            _resident((1, d)),
            _resident(wup.shape),
            _resident(cwr.shape),
            _resident(cbr.shape),
            _resident(wdn.shape),
            _resident((1, d)),
        ],
        out_specs=pl.BlockSpec((1, tm, d), lambda bi, i: (bi, i, 0)),
        out_shape=jax.ShapeDtypeStruct((b, l, d), F32),
        scratch_shapes=[pltpu.VMEM((tm + 2 * HALO, d), BF16), pltpu.VMEM((tm, d_ff), BF16)],
        compiler_params=pltpu.CompilerParams(
            dimension_semantics=("parallel", "arbitrary"), vmem_limit_bytes=_vmem_limit(vmem)),
        name="conv_ffn",
    )(x, x, x, norm_g.reshape(1, d), wup, cwr, cbr, wdn, final_g.reshape(1, d))


def _split3(x):
    hi = x.astype(BF16)
    r = x - hi.astype(F32)
    mid = r.astype(BF16)
    lo = (r - mid.astype(F32)).astype(BF16)
    return hi, mid, lo


def _proj_kernel(x_ref, g_ref, w_ref, wg_ref, bg_ref, pre_ref, suf_ref,
                 q_ref, k_ref, qt_ref, kt_ref, v_ref, gc_ref, gr_ref, *, mq, mv, heads, k_scale):
    h = _rmsnorm(x_ref[0], g_ref[...]).astype(BF16)
    q = jnp.dot(h, w_ref[:, 0:mq], preferred_element_type=F32)
    k = jnp.dot(h, w_ref[:, mq:2 * mq], preferred_element_type=F32) * k_scale
    q_ref[0] = q.astype(BF16)
    k_ref[0] = k.astype(BF16)
    qt_ref[0] = q.T.astype(BF16)
    kt_ref[0] = k.T.astype(BF16)
    v_ref[0] = jnp.dot(h, w_ref[:, 2 * mq:2 * mq + mv], preferred_element_type=F32).astype(BF16)
    gt = jnp.dot(h, wg_ref[...], preferred_element_type=F32) + bg_ref[...]
    lane = lax.broadcasted_iota(jnp.int32, gt.shape, 1)
    kind = lane // heads
    log_sig = jnp.minimum(gt, 0.0) - jnp.log1p(jnp.exp(-jnp.abs(gt)))
    parts = _split3(log_sig)
    t = pre_ref.shape[0]
    pre, suf = [], []
    for c in range(gt.shape[0] // t):
        rows = slice(c * t, (c + 1) * t)
        pre.append(sum(jnp.dot(pre_ref[...], p[rows], preferred_element_type=F32) for p in parts))
        suf.append(sum(jnp.dot(suf_ref[...], p[rows], preferred_element_type=F32) for p in parts))
    gt = jnp.where(kind == 1, jnp.concatenate(pre, axis=0), jnp.where(kind == 3, jnp.concatenate(suf, axis=0), gt))
    gc_ref[0] = gt
    gr_ref[0] = gt.T[0:gr_ref.shape[1], :]


def _mlstm_proj(x, norm_g, w_in, b_gate, *, heads, mv):
    b, l, d = x.shape
    ng = N_GATE_KINDS * heads
    mq = (w_in.shape[1] - 2 * mv - ng) // 2
    dqk = mq // heads
    tm = min(1024, l)
    assert l % tm == 0 and ng <= V7X_LANES
    w_main = w_in[:, :2 * mq + mv].astype(BF16)
    w_gate = jnp.pad(w_in[:, 2 * mq + 2 * mv:], ((0, 0), (0, V7X_LANES - ng))).astype(BF16)
    b_pad = jnp.pad(b_gate, (0, V7X_LANES - ng)).reshape(1, V7X_LANES)
    t = min(SCAN_CHUNK, l)
    assert tm % t == 0
    pos = np.arange(t)
    pre_m = jnp.asarray(pos[None, :] <= pos[:, None], BF16)
    suf_m = jnp.asarray(pos[None, :] >= pos[:, None], BF16)
    kern = functools.partial(_proj_kernel, mq=mq, mv=mv, heads=heads, k_scale=float(dqk) ** -0.5)
    row = lambda bi, i: (bi, i, 0)
    col = lambda bi, i: (bi, 0, i)
    vmem = (w_main.size * 2 + w_gate.size * 2 + 2 * tm * d * 4 + 2 * tm * (4 * mq + mv) * 2 + 8 * tm * mv * 4)
    return pl.pallas_call(
        kern,
        grid=(b, l // tm),
        in_specs=[
            pl.BlockSpec((1, tm, d), row),
            _resident((1, d)),
            _resident(w_main.shape),
            _resident(w_gate.shape),
            _resident((1, V7X_LANES)),
            _resident((t, t)),
            _resident((t, t)),
        ],
        out_specs=[
            pl.BlockSpec((1, tm, mq), row),
            pl.BlockSpec((1, tm, mq), row),
            pl.BlockSpec((1, mq, tm), col),
            pl.BlockSpec((1, mq, tm), col),
            pl.BlockSpec((1, tm, mv), row),
            pl.BlockSpec((1, tm, V7X_LANES), row),
            pl.BlockSpec((1, ng, tm), col),
        ],
        out_shape=[
            jax.ShapeDtypeStruct((b, l, mq), BF16),
            jax.ShapeDtypeStruct((b, l, mq), BF16),
            jax.ShapeDtypeStruct((b, mq, l), BF16),
            jax.ShapeDtypeStruct((b, mq, l), BF16),
            jax.ShapeDtypeStruct((b, l, mv), BF16),
            jax.ShapeDtypeStruct((b, l, V7X_LANES), F32),
            jax.ShapeDtypeStruct((b, ng, l), F32),
        ],
        compiler_params=pltpu.CompilerParams(
            dimension_semantics=("parallel", "parallel"), vmem_limit_bytes=_vmem_limit(vmem)),
        name="mlstm_proj",
    )(x, norm_g.reshape(1, d), w_main, w_gate, b_pad, pre_m, suf_m)


def _scan_kernel(qf_ref, kf_ref, qtf_ref, ktf_ref, vf_ref, gcf_ref, grf_ref,
                 qb_ref, kb_ref, qtb_ref, ktb_ref, vb_ref, gcb_ref, grb_ref,
                 hf_ref, hb_ref, c_ref, rhs_ref, n_ref, m_ref, *, t, heads, dqk, dv):
    step = pl.program_id(1)
    par = step % 2

    @pl.when(step == 0)
    def _():
        c_ref[...] = jnp.zeros_like(c_ref)
        rhs_ref[...] = jnp.zeros_like(rhs_ref)
        n_ref[...] = jnp.zeros_like(n_ref)
        m_ref[...] = jnp.zeros_like(m_ref)

    src_i = lax.broadcasted_iota(jnp.int32, (t, t), 0)
    tgt_i = lax.broadcasted_iota(jnp.int32, (t, t), 1)
    first_head = lax.broadcasted_iota(jnp.int32, (t, 2 * dqk), 1) < dqk
    n_row = lax.broadcasted_iota(jnp.int32, (N_ROWS, 2 * t), 0)
    n_lane_first = lax.broadcasted_iota(jnp.int32, (N_ROWS, 2 * t), 1) < t
    own_block = ((n_row == 0) & n_lane_first) | ((n_row == 1) & jnp.logical_not(n_lane_first))
    n_row_q = lax.broadcasted_iota(jnp.int32, (N_ROWS, 2 * dqk), 0)
    fwd_refs = (qf_ref, kf_ref, qtf_ref, ktf_ref, vf_ref, gcf_ref, grf_ref, hf_ref)
    bwd_refs = (qb_ref, kb_ref, qtb_ref, ktb_ref, vb_ref, gcb_ref, grb_ref, hb_ref)
    streams = []
    for bb in range(hf_ref.shape[0]):
        streams.append(tuple(r.at[bb:bb + 1] for r in fwd_refs) + (src_i <= tgt_i, 0, t - 1))
        streams.append(tuple(r.at[bb:bb + 1] for r in bwd_refs) + (src_i >= tgt_i, 2 * heads, 0))
    dn_nt = (((1,), (1,)), ((), ()))
    dn_tn = (((0,), (0,)), ((), ()))
    half = heads // 2

    def block_diag(a2):
        zero = jnp.zeros_like(a2)
        return jnp.concatenate([jnp.where(first_head, a2, zero), jnp.where(first_head, zero, a2)], axis=0)

    every = [(di, h, di * heads + h) for di in range(len(streams)) for h in range(heads)]
    pairs = [(di, p, di * half + p) for di in range(len(streams)) for p in range(half)]
    st = {}
    for di, h, idx in every:
        gc_ref, gr_ref, base, end = streams[di][5], streams[di][6], streams[di][9], streams[di][10]
        ig_r = gr_ref[0, base + h:base + h + 1, :]
        b_r = gr_ref[0, base + heads + h:base + heads + h + 1, :]
        total = b_r[:, end:end + 1]
        m_prev = m_ref[idx]
        st[idx] = dict(b_r=b_r, inter=b_r + m_prev, gk=total - b_r + ig_r, carry=total + m_prev,
                       src_c=gc_ref[0, :, base + h:base + h + 1] - gc_ref[0, :, base + heads + h:base + heads + h + 1])
    for di, h, idx in every:
        e = st[idx]
        m_new = jnp.maximum(e["carry"], jnp.max(e["gk"], axis=-1, keepdims=True))
        m_ref[idx] = m_new
        e["wk"] = jnp.exp(e["gk"] - m_new)
        e["decay"] = jnp.exp(e["carry"] - m_new)

    sp = {}
    for di, p, pi in pairs:
        sl = slice(2 * p * dqk, (2 * p + 2) * dqk)
        sp[pi] = dict(k2=streams[di][1][0, :, sl], n_prev=n_ref[pi])
        lhs = jnp.concatenate([sp[pi]["k2"], sp[pi]["n_prev"].astype(BF16)], axis=0)
        s2 = lax.dot_general(lhs, block_diag(streams[di][0][0, :, sl]), dn_nt, preferred_element_type=F32)
        for j in range(2):
            e = st[di * heads + 2 * p + j]
            e["qk"] = s2[:t, j * t:(j + 1) * t]
            e["qn"] = s2[t + j:t + j + 1, j * t:(j + 1) * t]

    for di, h, idx in every:
        e = st[idx]
        e["v"] = streams[di][4][0, :, h * dv:(h + 1) * dv]
        e["kwt"] = (streams[di][3][0, h * dqk:(h + 1) * dqk, :].astype(F32) * e["wk"]).astype(BF16)
    for di, h, idx in every:
        e = st[idx]
        e["upd"] = jnp.dot(e["kwt"], e["v"], preferred_element_type=F32)
        rhs_ref[par, idx, :t, :] = e["v"]
    for di, p, pi in pairs:
        ea, eb = st[di * heads + 2 * p], st[di * heads + 2 * p + 1]
        wk2 = jnp.concatenate([ea["wk"], eb["wk"]], axis=1)
        w_rows = jnp.where(own_block, wk2, 0.0).astype(BF16)
        dec = jnp.where(n_row_q == 0, ea["decay"], eb["decay"])
        n_ref[pi] = dec * sp[pi]["n_prev"] + jnp.dot(w_rows, block_diag(sp[pi]["k2"]), preferred_element_type=F32)
    for di, h, idx in every:
        e = st[idx]
        c_new = e["decay"] * c_ref[idx] + e["upd"]
        c_ref[idx] = c_new
        rhs_ref[1 - par, idx, t:, :] = c_new.astype(BF16)

    for di, h, idx in every:
        e = st[idx]
        e["dlog"] = jnp.where(streams[di][8], e["src_c"] + e["b_r"], -jnp.inf)
    for di, h, idx in every:
        e = st[idx]
        e["m_t"] = jnp.maximum(e["inter"], jnp.max(e["dlog"], axis=0, keepdims=True))
    for di, h, idx in every:
        e = st[idx]
        e["s"] = e["qk"] * jnp.exp(e["dlog"] - e["m_t"])
        e["sc"] = jnp.exp(e["inter"] - e["m_t"])
    for di, h, idx in every:
        e = st[idx]
        den = jnp.sum(e["s"], axis=0, keepdims=True) + e["sc"] * e["qn"]
        e["r"] = 1.0 / jnp.maximum(jnp.abs(den), jnp.exp(-e["m_t"]))
    for di, h, idx in every:
        e = st[idx]
        qt32 = streams[di][2][0, h * dqk:(h + 1) * dqk, :].astype(F32)
        e["lhs_t"] = jnp.concatenate(
            [(e["s"] * e["r"]).astype(BF16), (qt32 * (e["sc"] * e["r"])).astype(BF16)], axis=0)
    for di, h, idx in every:
        streams[di][7][0, :, h * dv:(h + 1) * dv] = lax.dot_general(
            st[idx]["lhs_t"], rhs_ref[par, idx], dn_tn, preferred_element_type=F32).astype(streams[di][7].dtype)


def _mlstm_scan(q, k, qt, kt, v, gc, gr, *, heads):
    b, l, mq = q.shape
    mv = v.shape[-1]
    dqk, dv = mq // heads, mv // heads
    t = min(SCAN_CHUNK, l)
    assert l % t == 0 and heads % 2 == 0
    nc = l // t
    ng = gr.shape[1]
    kern = functools.partial(_scan_kernel, t=t, heads=heads, dqk=dqk, dv=dv)
    fwd = lambda bi, c: (bi, c, 0)
    bwd = lambda bi, c: (bi, nc - 1 - c, 0)
    fwd_r = lambda bi, c: (bi, 0, c)
    bwd_r = lambda bi, c: (bi, 0, nc - 1 - c)

    rows = SCAN_ROWS if b % SCAN_ROWS == 0 else 1
    chains = 2 * rows * heads

    def stream(tok, rowm):
        return [pl.BlockSpec((rows, t, mq), tok), pl.BlockSpec((rows, t, mq), tok),
                pl.BlockSpec((rows, mq, t), rowm), pl.BlockSpec((rows, mq, t), rowm),
                pl.BlockSpec((rows, t, mv), tok),
                pl.BlockSpec((rows, t, V7X_LANES), tok), pl.BlockSpec((rows, ng, t), rowm)]

    return pl.pallas_call(
        kern,
        grid=(b // rows, nc),
        in_specs=stream(fwd, fwd_r) + stream(bwd, bwd_r),
        out_specs=[pl.BlockSpec((rows, t, mv), fwd), pl.BlockSpec((rows, t, mv), bwd)],
        out_shape=[jax.ShapeDtypeStruct((b, l, mv), BF16), jax.ShapeDtypeStruct((b, l, mv), BF16)],
        scratch_shapes=[pltpu.VMEM((chains, dqk, dv), F32), pltpu.VMEM((2, chains, t + dqk, dv), BF16),
                        pltpu.VMEM((chains // 2, N_ROWS, 2 * dqk), F32), pltpu.VMEM((chains, 1, 1), F32)],
        compiler_params=pltpu.CompilerParams(dimension_semantics=("parallel", "arbitrary")),
        name="mlstm_scan",
    )(q, k, qt, kt, v, gc, gr, q, k, qt, kt, v, gc, gr)


def _mout_kernel(hf_ref, hb_ref, x_ref, g_ref, wo_ref, hg_ref, w_ref, out_ref, y_ref, *, heads, dv):
    x = x_ref[0]
    o = jnp.dot(_rmsnorm(x, g_ref[...]).astype(BF16), wo_ref[...], preferred_element_type=F32)
    for h in range(heads):
        sl = slice(h * dv, (h + 1) * dv)
        hs = hf_ref[0, :, sl].astype(F32) + hb_ref[0, :, sl].astype(F32)
        hs = hs * lax.rsqrt(jnp.mean(hs * hs, axis=-1, keepdims=True) + EPS)
        hs = hs * hg_ref[:, sl]
        y_ref[:, sl] = (hs * jax.nn.sigmoid(o[:, sl])).astype(BF16)
    out_ref[0] = x + jnp.dot(y_ref[...], w_ref[...], preferred_element_type=F32)


def _mlstm_out(hf, hb, x, norm_g, w_gate_out, head_g, w_out):
    b, l, d = x.shape
    heads, dv = head_g.shape
    mv = heads * dv
    tm = min(1024, l)
    assert l % tm == 0
    row = lambda bi, i: (bi, i, 0)
    w = w_out.astype(BF16)
    wo = w_gate_out.astype(BF16)
    vmem = (w.size + wo.size) * 2 + 2 * tm * (2 * mv * 2 + 2 * d * 4) + tm * mv * 2 + 8 * tm * d * 4
    return pl.pallas_call(
        functools.partial(_mout_kernel, heads=heads, dv=dv),
        grid=(b, l // tm),
        in_specs=[pl.BlockSpec((1, tm, mv), row), pl.BlockSpec((1, tm, mv), row), pl.BlockSpec((1, tm, d), row),
                  _resident((1, d)), _resident(wo.shape), _resident((1, mv)), _resident(w.shape)],
        out_specs=pl.BlockSpec((1, tm, d), row),
        out_shape=jax.ShapeDtypeStruct((b, l, d), F32),
        scratch_shapes=[pltpu.VMEM((tm, mv), BF16)],
        compiler_params=pltpu.CompilerParams(
            dimension_semantics=("parallel", "parallel"), vmem_limit_bytes=_vmem_limit(vmem)),
        name="mlstm_out",
    )(hf, hb, x, norm_g.reshape(1, d), wo, head_g.reshape(1, mv), w)


def _dft_tables(l1, l2, gd, n2h, n2b):
    l = l1 * l2
    a1 = 2.0 * np.pi * np.outer(np.arange(l1), np.arange(l1)) / l1
    wa = np.stack([np.cos(a1), -np.sin(a1)]) / np.sqrt(l1)
    wa = np.einsum("rkn,ij->krinj", wa, np.eye(n2h)).reshape(l1 * 2 * n2h, l1 * n2h)
    tw = 2.0 * np.pi * np.einsum("k,jhi->jhki", np.arange(l1),
                                 np.arange(l2).reshape(l2 // n2b, n2b // n2h, n2h)) / l
    twr, twi = np.cos(tw)[..., None], -np.sin(tw)[..., None]
    a2 = 2.0 * np.pi * np.outer(np.arange(l2), np.arange(l2)) / l2
    c2, s2 = np.cos(a2), np.sin(a2)
    wc = np.block([[c2, s2], [-s2, c2]]) / np.sqrt(l2)
    ac = 2.0 * np.pi * np.outer(np.arange(gd), np.arange(gd)) / gd
    cs = np.stack([np.cos(ac), np.sin(ac)]) / np.sqrt(gd)
    return (jnp.asarray(wa, BF16), jnp.asarray(twr, F32), jnp.asarray(twi, F32),
            jnp.asarray(wc, BF16), jnp.asarray(cs, BF16))


def _fa_kernel(x_ref, g_ref, wa_ref, twr_ref, twi_ref, out_ref, *, l1, n2h, n2b, d):
    g = g_ref[...]
    wa = wa_ref[...]
    halves = []
    for hh in range(n2b // n2h):
        x = x_ref[0, :, hh * n2h:(hh + 1) * n2h, :].reshape(l1 * n2h, d)
        a = jnp.dot(wa, _rmsnorm(x, g).astype(BF16), preferred_element_type=F32)
        a = a.reshape(l1, 2, n2h, d)
        ar = a[:, 0]
        ai = a[:, 1]
        tr = twr_ref[0, hh]
        ti = twi_ref[0, hh]
        halves.append(jnp.stack([ar * tr - ai * ti, ar * ti + ai * tr], axis=1))
    out_ref[0] = jnp.concatenate(halves, axis=2).astype(BF16)


def _fc_kernel(ba_ref, x_ref, wc_ref, cs_ref, wout_ref, bout_ref, out_ref, z_ref, f_ref, *, l2, k1b, d, gd):
    wc = wc_ref[...]
    for j in range(k1b):
        z = jnp.dot(wc, ba_ref[0, j], preferred_element_type=F32)
        z_ref[0, j * l2:(j + 1) * l2, :] = z[:l2].astype(BF16)
        z_ref[1, j * l2:(j + 1) * l2, :] = z[l2:].astype(BF16)
    for gi in range(d // gd):
        sl = slice(gi * gd, (gi + 1) * gd)
        f = (jnp.dot(z_ref[0, :, sl], cs_ref[0], preferred_element_type=F32)
             + jnp.dot(z_ref[1, :, sl], cs_ref[1], preferred_element_type=F32))
        f_ref[:, sl] = f.astype(BF16)
    y = jnp.dot(f_ref[...], wout_ref[...], preferred_element_type=F32) + bout_ref[...]
    out_ref[0] = x_ref[0] + pltpu.einshape("jkd->kjd", y.reshape(k1b, l2, d))


def _fourier_mixer(x, norm_g, w_out, b_out):
    b, l, d = x.shape
    l2 = min(DFT_INNER, l)
    l1 = l // l2
    assert l1 * l2 == l and d % F_GROUPS == 0
    gd = d // F_GROUPS
    n2h = V7X_SUBLANES
    n2b = min(V7X_BF16_ROWS, l2)
    k1b = min(V7X_SUBLANES, l1)
    assert l2 % n2b == 0 and n2b % n2h == 0 and l1 % k1b == 0
    wa, twr, twi, wc, cs = _dft_tables(l1, l2, gd, n2h, n2b)

    tw_spec = pl.BlockSpec((1, n2b // n2h, l1, n2h, 1), lambda bi, j: (j, 0, 0, 0, 0))
    ba = pl.pallas_call(
        functools.partial(_fa_kernel, l1=l1, n2h=n2h, n2b=n2b, d=d),
        grid=(b, l2 // n2b),
        in_specs=[pl.BlockSpec((1, l1, n2b, d), lambda bi, j: (bi, 0, j, 0)),
                  _resident((1, d)), _resident(wa.shape), tw_spec, tw_spec],
        out_specs=pl.BlockSpec((1, l1, 2, n2b, d), lambda bi, j: (bi, 0, 0, j, 0)),
        out_shape=jax.ShapeDtypeStruct((b, l1, 2, l2, d), BF16),
        compiler_params=pltpu.CompilerParams(
            dimension_semantics=("parallel", "parallel"),
            vmem_limit_bytes=_vmem_limit(l1 * n2b * d * (2 * 4 + 2 * 2 * 2 + 6 * 4) + wa.size * 2)),
        name="fourier_seq_a",
    )(x.reshape(b, l1, l2, d), norm_g.reshape(1, d), wa, twr, twi)

    rows = k1b * l2
    out = pl.pallas_call(
        functools.partial(_fc_kernel, l2=l2, k1b=k1b, d=d, gd=gd),
        grid=(b, l1 // k1b),
        in_specs=[pl.BlockSpec((1, k1b, 2 * l2, d), lambda bi, j: (bi, j, 0, 0)),
                  pl.BlockSpec((1, l2, k1b, d), lambda bi, j: (bi, 0, j, 0)),
                  _resident(wc.shape), _resident(cs.shape), _resident((d, d)), _resident((1, d))],
        out_specs=pl.BlockSpec((1, l2, k1b, d), lambda bi, j: (bi, 0, j, 0)),
        out_shape=jax.ShapeDtypeStruct((b, l2, l1, d), F32),
        scratch_shapes=[pltpu.VMEM((2, rows, d), BF16), pltpu.VMEM((rows, d), BF16)],
        compiler_params=pltpu.CompilerParams(
            dimension_semantics=("parallel", "parallel"),
            vmem_limit_bytes=_vmem_limit(rows * d * (2 * 2 * 2 + 4 * 4 + 3 * 2 + 4 * 4) + 4 * d * d)),
        name="fourier_seq_c",
    )(ba.reshape(b, l1, 2 * l2, d), x.reshape(b, l2, l1, d), wc, cs, w_out.astype(BF16), b_out.reshape(1, d))
    return out.reshape(b, l, d)


def _mlstm_mixer(x, norm_g, w_in, b_gate, head_g, w_out):
    heads, dv = head_g.shape
    mv = heads * dv
    q, k, qt, kt, v, gc, gr = _mlstm_proj(x, norm_g, w_in, b_gate, heads=heads, mv=mv)
    hf, hb = _mlstm_scan(q, k, qt, kt, v, gc, gr, heads=heads)
    o_lo = 2 * q.shape[-1] + mv
    return _mlstm_out(hf, hb, x, norm_g, w_in[:, o_lo:o_lo + mv], head_g, w_out)


def _trunk(x, mix_norm_g, m_w_in, m_b_gate, m_head_g, m_w_out, f_w_out, f_b_out,
           ffn_norm_g, ffn_w_up, ffn_conv_w, ffn_conv_b, ffn_w_down, final_norm_g):
    depth = mix_norm_g.shape[0]
    n_mixers = 2
    for i in range(depth):
        j = i // n_mixers
        if i % n_mixers == 0:
            x = _mlstm_mixer(x, mix_norm_g[i], m_w_in[j], m_b_gate[j], m_head_g[j], m_w_out[j])
        else:
            x = _fourier_mixer(x, mix_norm_g[i], f_w_out[j], f_b_out[j])
        x = _conv_ffn(x, ffn_norm_g[i], ffn_w_up[i], ffn_conv_w[i], ffn_conv_b[i], ffn_w_down[i],
                      final_norm_g, final_norm=(i == depth - 1))
    return x


def kernel(x_prompt, x_sample, mix_norm_g, m_w_in, m_b_gate, m_head_g, m_w_out, f_w_out, f_b_out,
           ffn_norm_g, ffn_w_up, ffn_conv_w, ffn_conv_b, ffn_w_down, final_norm_g):
    params = (mix_norm_g, m_w_in, m_b_gate, m_head_g, m_w_out, f_w_out, f_b_out,
              ffn_norm_g, ffn_w_up, ffn_conv_w, ffn_conv_b, ffn_w_down, final_norm_g)
    return (_trunk(x_prompt, *params), _trunk(x_sample, *params))
```

```python
import functools

import numpy as np
import jax
import jax.numpy as jnp
from jax import lax
from jax.experimental import pallas as pl
from jax.experimental.pallas import tpu as pltpu

F32 = jnp.float32
BF16 = jnp.bfloat16

EPS = 1e-6
F_GROUPS = 4
N_GATE_KINDS = 4

V7X_VMEM_BYTES = 64 * 1024 * 1024
V7X_LANES = 128
V7X_SUBLANES = 8
V7X_BF16_ROWS = 16
V7X_MXU_DIM = 256

TOKEN_TILE = 1024
HALO = V7X_SUBLANES
SCAN_CHUNK = 128
N_ROWS = V7X_BF16_ROWS
SCAN_ROWS = 4
DFT_INNER = 128

VMEM_LIMIT_FLOOR = 32 * 1024 * 1024
VMEM_LIMIT_CEIL = V7X_VMEM_BYTES - 8 * 1024 * 1024


def _vmem_limit(nbytes):
    return int(min(max(nbytes, VMEM_LIMIT_FLOOR), VMEM_LIMIT_CEIL))


def _rmsnorm(x, g):
    y = x * lax.rsqrt(jnp.mean(x * x, axis=-1, keepdims=True) + EPS)
    return y * g


def _resident(shape):
    nd = len(shape)
    return pl.BlockSpec(shape, lambda *_: (0,) * nd, pipeline_mode=pl.Buffered(1))


def _ffn_kernel(xm_ref, xp_ref, xn_ref, g_ref, wup_ref, cw_ref, cb_ref, wdn_ref, fg_ref,
                o_ref, h_ref, act_ref, *, tm, n_chunks, fc, conv_w, final_norm):
    i = pl.program_id(1)
    last = pl.num_programs(1) - 1
    g = g_ref[...]
    x = xm_ref[0]
    hp = jnp.where(i > 0, _rmsnorm(xp_ref[0], g), 0.0)
    hn = jnp.where(i < last, _rmsnorm(xn_ref[0], g), 0.0)
    h_ref[0:HALO, :] = hp.astype(BF16)
    h_ref[HALO:HALO + tm, :] = _rmsnorm(x, g).astype(BF16)
    h_ref[HALO + tm:, :] = hn.astype(BF16)
    hb = h_ref[...]
    pad = conv_w // 2
    d_ff = n_chunks * fc

    def conv_cols(lo):
        u = jnp.dot(hb, wup_ref[:, lo:lo + fc], preferred_element_type=F32)
        y = cb_ref[:, lo:lo + fc]
        for j in [pad] + [j for j in range(conv_w) if j != pad]:
            off = HALO - pad + j
            y = y + u[off:off + tm] * cw_ref[j:j + 1, lo:lo + fc]
        return y

    for c in range(n_chunks):
        a = conv_cols(c * fc)
        val = conv_cols(d_ff + c * fc)
        act_ref[:, c * fc:(c + 1) * fc] = (a * jax.nn.sigmoid(a) * val).astype(BF16)
    r = x + jnp.dot(act_ref[...], wdn_ref[...], preferred_element_type=F32)
    if final_norm:
        r = _rmsnorm(r, fg_ref[...])
    o_ref[0] = r


def _conv_ffn(x, norm_g, w_up, conv_w, conv_b, w_down, final_g, *, final_norm):
    b, l, d = x.shape
    d_ff = w_down.shape[0]
    cw = conv_w.shape[0]
    fc = V7X_MXU_DIM
    n_chunks = d_ff // fc
    assert n_chunks * fc == d_ff and cw // 2 <= HALO
    tm = min(TOKEN_TILE, l)
    assert l % tm == 0 and tm % HALO == 0
    nt = l // tm
    rb = tm // HALO
    assert (2 * d_ff) % V7X_LANES == 0 and fc % V7X_LANES == 0
    wup = w_up.astype(BF16)
    cwr = conv_w
    cbr = conv_b.reshape(1, 2 * d_ff)
    wdn = w_down.astype(BF16)
    kern = functools.partial(_ffn_kernel, tm=tm, n_chunks=n_chunks, fc=fc, conv_w=cw, final_norm=final_norm)
    vmem = (wup.size * 2 + wdn.size * 2 + 4 * tm * d * 4 + (tm + 2 * HALO) * d * 2 + tm * d_ff * 2
            + 8 * (tm + 2 * HALO) * fc * 4 + 3 * tm * d * 4)
    return pl.pallas_call(
        kern,
        grid=(b, nt),
        in_specs=[
            pl.BlockSpec((1, tm, d), lambda bi, i: (bi, i, 0)),
            pl.BlockSpec((1, HALO, d), lambda bi, i: (bi, jnp.maximum(i * rb - 1, 0), 0)),
            pl.BlockSpec((1, HALO, d), lambda bi, i: (bi, jnp.minimum((i + 1) * rb, l // HALO - 1), 0)),
            _resident((1, d)),
            _resident(wup.shape),
            _resident(cwr.shape),
            _resident(cbr.shape),
            _resident(wdn.shape),
            _resident((1, d)),
        ],
        out_specs=pl.BlockSpec((1, tm, d), lambda bi, i: (bi, i, 0)),
        out_shape=jax.ShapeDtypeStruct((b, l, d), F32),
        scratch_shapes=[pltpu.VMEM((tm + 2 * HALO, d), BF16), pltpu.VMEM((tm, d_ff), BF16)],
        compiler_params=pltpu.CompilerParams(
            dimension_semantics=("parallel", "arbitrary"), vmem_limit_bytes=_vmem_limit(vmem)),
        name="conv_ffn",
    )(x, x, x, norm_g.reshape(1, d), wup, cwr, cbr, wdn, final_g.reshape(1, d))


def _split3(x):
    hi = x.astype(BF16)
    r = x - hi.astype(F32)
    mid = r.astype(BF16)
    lo = (r - mid.astype(F32)).astype(BF16)
    return hi, mid, lo


def _proj_kernel(x_ref, g_ref, w_ref, wg_ref, bg_ref, pre_ref,
                 q_ref, k_ref, qt_ref, kt_ref, v_ref, gc_ref, gr_ref, *, mq, mv, heads, k_scale):
    h = _rmsnorm(x_ref[0], g_ref[...]).astype(BF16)
    q = jnp.dot(h, w_ref[:, 0:mq], preferred_element_type=F32)
    k = jnp.dot(h, w_ref[:, mq:2 * mq], preferred_element_type=F32) * k_scale
    q_ref[0] = q.astype(BF16)
    k_ref[0] = k.astype(BF16)
    qt_ref[0] = q.T.astype(BF16)
    kt_ref[0] = k.T.astype(BF16)
    v_ref[0] = jnp.dot(h, w_ref[:, 2 * mq:2 * mq + mv], preferred_element_type=F32).astype(BF16)
    gt = jnp.dot(h, wg_ref[...], preferred_element_type=F32) + bg_ref[...]
    lane = lax.broadcasted_iota(jnp.int32, gt.shape, 1)
    kind = lane // heads
    log_sig = jnp.minimum(gt, 0.0) - jnp.log1p(jnp.exp(-jnp.abs(gt)))
    parts = _split3(log_sig)
    t = pre_ref.shape[0]
    pre, suf = [], []
    for c in range(gt.shape[0] // t):
        rows = slice(c * t, (c + 1) * t)
        p = sum(jnp.dot(pre_ref[...], part[rows], preferred_element_type=F32) for part in parts)
        pre.append(p)
        suf.append(p[t - 1:t, :] - p + log_sig[rows])
    gt = jnp.where(kind == 1, jnp.concatenate(pre, axis=0), jnp.where(kind == 3, jnp.concatenate(suf, axis=0), gt))
    gc_ref[0] = gt
    gr_ref[0] = gt.T[0:gr_ref.shape[1], :]


def _mlstm_proj(x, norm_g, w_in, b_gate, *, heads, mv):
    b, l, d = x.shape
    ng = N_GATE_KINDS * heads
    mq = (w_in.shape[1] - 2 * mv - ng) // 2
    dqk = mq // heads
    tm = min(TOKEN_TILE, l)
    assert l % tm == 0 and ng <= V7X_LANES
    w_main = w_in[:, :2 * mq + mv].astype(BF16)
    w_gate = jnp.pad(w_in[:, 2 * mq + 2 * mv:], ((0, 0), (0, V7X_LANES - ng))).astype(BF16)
    b_pad = jnp.pad(b_gate, (0, V7X_LANES - ng)).reshape(1, V7X_LANES)
    t = min(SCAN_CHUNK, l)
    assert tm % t == 0
    pos = np.arange(t)
    pre_m = jnp.asarray(pos[None, :] <= pos[:, None], BF16)
    kern = functools.partial(_proj_kernel, mq=mq, mv=mv, heads=heads, k_scale=float(dqk) ** -0.5)
    row = lambda bi, i: (bi, i, 0)
    col = lambda bi, i: (bi, 0, i)
    vmem = (w_main.size * 2 + w_gate.size * 2 + 2 * tm * d * 4 + 2 * tm * (4 * mq + mv) * 2 + 8 * tm * mv * 4)
    return pl.pallas_call(
        kern,
        grid=(b, l // tm),
        in_specs=[
            pl.BlockSpec((1, tm, d), row),
            _resident((1, d)),
            _resident(w_main.shape),
            _resident(w_gate.shape),
            _resident((1, V7X_LANES)),
            _resident((t, t)),
        ],
        out_specs=[
            pl.BlockSpec((1, tm, mq), row),
            pl.BlockSpec((1, tm, mq), row),
            pl.BlockSpec((1, mq, tm), col),
            pl.BlockSpec((1, mq, tm), col),
            pl.BlockSpec((1, tm, mv), row),
            pl.BlockSpec((1, tm, V7X_LANES), row),
            pl.BlockSpec((1, ng, tm), col),
        ],
        out_shape=[
            jax.ShapeDtypeStruct((b, l, mq), BF16),
            jax.ShapeDtypeStruct((b, l, mq), BF16),
            jax.ShapeDtypeStruct((b, mq, l), BF16),
            jax.ShapeDtypeStruct((b, mq, l), BF16),
            jax.ShapeDtypeStruct((b, l, mv), BF16),
            jax.ShapeDtypeStruct((b, l, V7X_LANES), F32),
            jax.ShapeDtypeStruct((b, ng, l), F32),
        ],
        compiler_params=pltpu.CompilerParams(
            dimension_semantics=("parallel", "parallel"), vmem_limit_bytes=_vmem_limit(vmem)),
        name="mlstm_proj",
    )(x, norm_g.reshape(1, d), w_main, w_gate, b_pad, pre_m)


def _scan_kernel(qf_ref, kf_ref, qtf_ref, ktf_ref, vf_ref, gcf_ref, grf_ref,
                 qb_ref, kb_ref, qtb_ref, ktb_ref, vb_ref, gcb_ref, grb_ref,
                 hf_ref, hb_ref, c_ref, rhs_ref, n_ref, m_ref, *, t, heads, dqk, dv):
    step = pl.program_id(1)
    par = step % 2

    @pl.when(step == 0)
    def _():
        c_ref[...] = jnp.zeros_like(c_ref)
        rhs_ref[...] = jnp.zeros_like(rhs_ref)
        n_ref[...] = jnp.zeros_like(n_ref)
        m_ref[...] = jnp.zeros_like(m_ref)

    src_i = lax.broadcasted_iota(jnp.int32, (t, t), 0)
    tgt_i = lax.broadcasted_iota(jnp.int32, (t, t), 1)
    first_head = lax.broadcasted_iota(jnp.int32, (t, 2 * dqk), 1) < dqk
    n_row = lax.broadcasted_iota(jnp.int32, (N_ROWS, 2 * t), 0)
    n_lane_first = lax.broadcasted_iota(jnp.int32, (N_ROWS, 2 * t), 1) < t
    own_block = ((n_row == 0) & n_lane_first) | ((n_row == 1) & jnp.logical_not(n_lane_first))
    n_row_q = lax.broadcasted_iota(jnp.int32, (N_ROWS, 2 * dqk), 0)
    fwd_refs = (qf_ref, kf_ref, qtf_ref, ktf_ref, vf_ref, gcf_ref, grf_ref, hf_ref)
    bwd_refs = (qb_ref, kb_ref, qtb_ref, ktb_ref, vb_ref, gcb_ref, grb_ref, hb_ref)
    streams = []
    for bb in range(hf_ref.shape[0]):
        streams.append(tuple(r.at[bb:bb + 1] for r in fwd_refs) + (src_i <= tgt_i, 0, t - 1))
        streams.append(tuple(r.at[bb:bb + 1] for r in bwd_refs) + (src_i >= tgt_i, 2 * heads, 0))
    dn_nt = (((1,), (1,)), ((), ()))
    dn_tn = (((0,), (0,)), ((), ()))
    half = heads // 2

    def block_diag(a2):
        zero = jnp.zeros_like(a2)
        return jnp.concatenate([jnp.where(first_head, a2, zero), jnp.where(first_head, zero, a2)], axis=0)

    every = [(di, h, di * heads + h) for di in range(len(streams)) for h in range(heads)]
    pairs = [(di, p, di * half + p) for di in range(len(streams)) for p in range(half)]
    st = {}
    for di, h, idx in every:
        gc_ref, gr_ref, base, end = streams[di][5], streams[di][6], streams[di][9], streams[di][10]
        ig_r = gr_ref[0, base + h:base + h + 1, :]
        b_r = gr_ref[0, base + heads + h:base + heads + h + 1, :]
        total = b_r[:, end:end + 1]
        m_prev = m_ref[idx]
        st[idx] = dict(b_r=b_r, inter=b_r + m_prev, gk=total - b_r + ig_r, carry=total + m_prev,
                       src_c=gc_ref[0, :, base + h:base + h + 1] - gc_ref[0, :, base + heads + h:base + heads + h + 1])
    for di, h, idx in every:
        e = st[idx]
        m_new = jnp.maximum(e["carry"], jnp.max(e["gk"], axis=-1, keepdims=True))
        m_ref[idx] = m_new
        e["wk"] = jnp.exp(e["gk"] - m_new)
        e["decay"] = jnp.exp(e["carry"] - m_new)

    sp = {}
    for di, p, pi in pairs:
        sl = slice(2 * p * dqk, (2 * p + 2) * dqk)
        sp[pi] = dict(k2=streams[di][1][0, :, sl], n_prev=n_ref[pi])
        lhs = jnp.concatenate([sp[pi]["k2"], sp[pi]["n_prev"].astype(BF16)], axis=0)
        s2 = lax.dot_general(lhs, block_diag(streams[di][0][0, :, sl]), dn_nt, preferred_element_type=F32)
        for j in range(2):
            e = st[di * heads + 2 * p + j]
            e["qk"] = s2[:t, j * t:(j + 1) * t]
            e["qn"] = s2[t + j:t + j + 1, j * t:(j + 1) * t]

    for di, h, idx in every:
        e = st[idx]
        e["v"] = streams[di][4][0, :, h * dv:(h + 1) * dv]
        e["kwt"] = (streams[di][3][0, h * dqk:(h + 1) * dqk, :].astype(F32) * e["wk"]).astype(BF16)
    for di, h, idx in every:
        e = st[idx]
        e["upd"] = jnp.dot(e["kwt"], e["v"], preferred_element_type=F32)
        rhs_ref[par, idx, :t, :] = e["v"]
    for di, p, pi in pairs:
        ea, eb = st[di * heads + 2 * p], st[di * heads + 2 * p + 1]
        wk2 = jnp.concatenate([ea["wk"], eb["wk"]], axis=1)
        w_rows = jnp.where(own_block, wk2, 0.0).astype(BF16)
        dec = jnp.where(n_row_q == 0, ea["decay"], eb["decay"])
        n_ref[pi] = dec * sp[pi]["n_prev"] + jnp.dot(w_rows, block_diag(sp[pi]["k2"]), preferred_element_type=F32)
    for di, h, idx in every:
        e = st[idx]
        c_new = e["decay"] * c_ref[idx] + e["upd"]
        c_ref[idx] = c_new
        rhs_ref[1 - par, idx, t:, :] = c_new.astype(BF16)

    for di, h, idx in every:
        e = st[idx]
        e["dlog"] = jnp.where(streams[di][8], e["src_c"] + e["b_r"], -jnp.inf)
    for di, h, idx in every:
        e = st[idx]
        e["m_t"] = jnp.maximum(e["inter"], jnp.max(e["dlog"], axis=0, keepdims=True))
    for di, h, idx in every:
        e = st[idx]
        e["s"] = e["qk"] * jnp.exp(e["dlog"] - e["m_t"])
        e["sc"] = jnp.exp(e["inter"] - e["m_t"])
    for di, h, idx in every:
        e = st[idx]
        den = jnp.sum(e["s"], axis=0, keepdims=True) + e["sc"] * e["qn"]
        e["r"] = 1.0 / jnp.maximum(jnp.abs(den), jnp.exp(-e["m_t"]))
    for di, h, idx in every:
        e = st[idx]
        qt32 = streams[di][2][0, h * dqk:(h + 1) * dqk, :].astype(F32)
        e["lhs_t"] = jnp.concatenate(
            [(e["s"] * e["r"]).astype(BF16), (qt32 * (e["sc"] * e["r"])).astype(BF16)], axis=0)
    for di, h, idx in every:
        streams[di][7][0, :, h * dv:(h + 1) * dv] = lax.dot_general(
            st[idx]["lhs_t"], rhs_ref[par, idx], dn_tn, preferred_element_type=F32).astype(streams[di][7].dtype)


def _mlstm_scan(q, k, qt, kt, v, gc, gr, *, heads):
    b, l, mq = q.shape
    mv = v.shape[-1]
    dqk, dv = mq // heads, mv // heads
    t = min(SCAN_CHUNK, l)
    assert l % t == 0 and heads % 2 == 0
    nc = l // t
    ng = gr.shape[1]
    kern = functools.partial(_scan_kernel, t=t, heads=heads, dqk=dqk, dv=dv)
    fwd = lambda bi, c: (bi, c, 0)
    bwd = lambda bi, c: (bi, nc - 1 - c, 0)
    fwd_r = lambda bi, c: (bi, 0, c)
    bwd_r = lambda bi, c: (bi, 0, nc - 1 - c)

    rows = SCAN_ROWS if b % SCAN_ROWS == 0 else 1
    chains = 2 * rows * heads

    def stream(tok, rowm):
        return [pl.BlockSpec((rows, t, mq), tok), pl.BlockSpec((rows, t, mq), tok),
                pl.BlockSpec((rows, mq, t), rowm), pl.BlockSpec((rows, mq, t), rowm),
                pl.BlockSpec((rows, t, mv), tok),
                pl.BlockSpec((rows, t, V7X_LANES), tok), pl.BlockSpec((rows, ng, t), rowm)]

    return pl.pallas_call(
        kern,
        grid=(b // rows, nc),
        in_specs=stream(fwd, fwd_r) + stream(bwd, bwd_r),
        out_specs=[pl.BlockSpec((rows, t, mv), fwd), pl.BlockSpec((rows, t, mv), bwd)],
        out_shape=[jax.ShapeDtypeStruct((b, l, mv), BF16), jax.ShapeDtypeStruct((b, l, mv), BF16)],
        scratch_shapes=[pltpu.VMEM((chains, dqk, dv), F32), pltpu.VMEM((2, chains, t + dqk, dv), BF16),
                        pltpu.VMEM((chains // 2, N_ROWS, 2 * dqk), F32), pltpu.VMEM((chains, 1, 1), F32)],
        compiler_params=pltpu.CompilerParams(dimension_semantics=("parallel", "arbitrary")),
        name="mlstm_scan",
    )(q, k, qt, kt, v, gc, gr, q, k, qt, kt, v, gc, gr)


def _mout_kernel(hf_ref, hb_ref, x_ref, g_ref, wo_ref, hg_ref, w_ref, out_ref, y_ref, *, heads, dv):
    x = x_ref[0]
    o = jnp.dot(_rmsnorm(x, g_ref[...]).astype(BF16), wo_ref[...], preferred_element_type=F32)
    for h in range(heads):
        sl = slice(h * dv, (h + 1) * dv)
        hs = hf_ref[0, :, sl].astype(F32) + hb_ref[0, :, sl].astype(F32)
        hs = hs * lax.rsqrt(jnp.mean(hs * hs, axis=-1, keepdims=True) + EPS)
        hs = hs * hg_ref[:, sl]
        y_ref[:, sl] = (hs * jax.nn.sigmoid(o[:, sl])).astype(BF16)
    out_ref[0] = x + jnp.dot(y_ref[...], w_ref[...], preferred_element_type=F32)


def _mlstm_out(hf, hb, x, norm_g, w_gate_out, head_g, w_out):
    b, l, d = x.shape
    heads, dv = head_g.shape
    mv = heads * dv
    tm = min(TOKEN_TILE, l)
    assert l % tm == 0
    row = lambda bi, i: (bi, i, 0)
    w = w_out.astype(BF16)
    wo = w_gate_out.astype(BF16)
    vmem = (w.size + wo.size) * 2 + 2 * tm * (2 * mv * 2 + 2 * d * 4) + tm * mv * 2 + 8 * tm * d * 4
    return pl.pallas_call(
        functools.partial(_mout_kernel, heads=heads, dv=dv),
        grid=(b, l // tm),
        in_specs=[pl.BlockSpec((1, tm, mv), row), pl.BlockSpec((1, tm, mv), row), pl.BlockSpec((1, tm, d), row),
                  _resident((1, d)), _resident(wo.shape), _resident((1, mv)), _resident(w.shape)],
        out_specs=pl.BlockSpec((1, tm, d), row),
        out_shape=jax.ShapeDtypeStruct((b, l, d), F32),
        scratch_shapes=[pltpu.VMEM((tm, mv), BF16)],
        compiler_params=pltpu.CompilerParams(
            dimension_semantics=("parallel", "parallel"), vmem_limit_bytes=_vmem_limit(vmem)),
        name="mlstm_out",
    )(hf, hb, x, norm_g.reshape(1, d), wo, head_g.reshape(1, mv), w)


def _dft_tables(l1, l2, gd, n2h, n2b):
    l = l1 * l2
    a1 = 2.0 * np.pi * np.outer(np.arange(l1), np.arange(l1)) / l1
    wa = np.stack([np.cos(a1), -np.sin(a1)]) / np.sqrt(l1)
    wa = np.einsum("rkn,ij->krinj", wa, np.eye(n2h)).reshape(l1 * 2 * n2h, l1 * n2h)
    tw = 2.0 * np.pi * np.einsum("k,jhi->jhki", np.arange(l1),
                                 np.arange(l2).reshape(l2 // n2b, n2b // n2h, n2h)) / l
    twr, twi = np.cos(tw)[..., None], -np.sin(tw)[..., None]
    a2 = 2.0 * np.pi * np.outer(np.arange(l2), np.arange(l2)) / l2
    c2, s2 = np.cos(a2), np.sin(a2)
    wc = np.block([[c2, s2], [-s2, c2]]) / np.sqrt(l2)
    ac = 2.0 * np.pi * np.outer(np.arange(gd), np.arange(gd)) / gd
    cs = np.stack([np.cos(ac), np.sin(ac)]) / np.sqrt(gd)
    return (jnp.asarray(wa, BF16), jnp.asarray(twr, F32), jnp.asarray(twi, F32),
            jnp.asarray(wc, BF16), jnp.asarray(cs, BF16))


def _fa_kernel(x_ref, g_ref, wa_ref, twr_ref, twi_ref, out_ref, *, l1, n2h, n2b, d):
    g = g_ref[...]
    wa = wa_ref[...]
    halves = []
    for hh in range(n2b // n2h):
        x = x_ref[0, :, hh * n2h:(hh + 1) * n2h, :].reshape(l1 * n2h, d)
        a = jnp.dot(wa, _rmsnorm(x, g).astype(BF16), preferred_element_type=F32)
        a = a.reshape(l1, 2, n2h, d)
        ar = a[:, 0]
        ai = a[:, 1]
        tr = twr_ref[0, hh]
        ti = twi_ref[0, hh]
        halves.append(jnp.stack([ar * tr - ai * ti, ar * ti + ai * tr], axis=1))
    out_ref[0] = jnp.concatenate(halves, axis=2).astype(BF16)


def _fc_kernel(ba_ref, x_ref, wc_ref, cs_ref, wout_ref, bout_ref, out_ref, z_ref, f_ref, *, l2, k1b, d, gd):
    wc = wc_ref[...]
    for j in range(k1b):
        z = jnp.dot(wc, ba_ref[0, j], preferred_element_type=F32)
        z_ref[0, j * l2:(j + 1) * l2, :] = z[:l2].astype(BF16)
        z_ref[1, j * l2:(j + 1) * l2, :] = z[l2:].astype(BF16)
    for gi in range(d // gd):
        sl = slice(gi * gd, (gi + 1) * gd)
        f = (jnp.dot(z_ref[0, :, sl], cs_ref[0], preferred_element_type=F32)
             + jnp.dot(z_ref[1, :, sl], cs_ref[1], preferred_element_type=F32))
        f_ref[:, sl] = f.astype(BF16)
    y = jnp.dot(f_ref[...], wout_ref[...], preferred_element_type=F32) + bout_ref[...]
    out_ref[0] = x_ref[0] + pltpu.einshape("jkd->kjd", y.reshape(k1b, l2, d))


def _fourier_mixer(x, norm_g, w_out, b_out):
    b, l, d = x.shape
    l2 = min(DFT_INNER, l)
    l1 = l // l2
    assert l1 * l2 == l and d % F_GROUPS == 0
    gd = d // F_GROUPS
    n2h = V7X_SUBLANES
    n2b = min(V7X_BF16_ROWS, l2)
    k1b = min(V7X_SUBLANES, l1)
    assert l2 % n2b == 0 and n2b % n2h == 0 and l1 % k1b == 0
    wa, twr, twi, wc, cs = _dft_tables(l1, l2, gd, n2h, n2b)

    tw_spec = pl.BlockSpec((1, n2b // n2h, l1, n2h, 1), lambda bi, j: (j, 0, 0, 0, 0))
    ba = pl.pallas_call(
        functools.partial(_fa_kernel, l1=l1, n2h=n2h, n2b=n2b, d=d),
        grid=(b, l2 // n2b),
        in_specs=[pl.BlockSpec((1, l1, n2b, d), lambda bi, j: (bi, 0, j, 0)),
                  _resident((1, d)), _resident(wa.shape), tw_spec, tw_spec],
        out_specs=pl.BlockSpec((1, l1, 2, n2b, d), lambda bi, j: (bi, 0, 0, j, 0)),
        out_shape=jax.ShapeDtypeStruct((b, l1, 2, l2, d), BF16),
        compiler_params=pltpu.CompilerParams(
            dimension_semantics=("parallel", "parallel"),
            vmem_limit_bytes=_vmem_limit(l1 * n2b * d * (2 * 4 + 2 * 2 * 2 + 6 * 4) + wa.size * 2)),
        name="fourier_seq_a",
    )(x.reshape(b, l1, l2, d), norm_g.reshape(1, d), wa, twr, twi)

    rows = k1b * l2
    out = pl.pallas_call(
        functools.partial(_fc_kernel, l2=l2, k1b=k1b, d=d, gd=gd),
        grid=(b, l1 // k1b),
        in_specs=[pl.BlockSpec((1, k1b, 2 * l2, d), lambda bi, j: (bi, j, 0, 0)),
                  pl.BlockSpec((1, l2, k1b, d), lambda bi, j: (bi, 0, j, 0)),
                  _resident(wc.shape), _resident(cs.shape), _resident((d, d)), _resident((1, d))],
        out_specs=pl.BlockSpec((1, l2, k1b, d), lambda bi, j: (bi, 0, j, 0)),
        out_shape=jax.ShapeDtypeStruct((b, l2, l1, d), F32),
        scratch_shapes=[pltpu.VMEM((2, rows, d), BF16), pltpu.VMEM((rows, d), BF16)],
        compiler_params=pltpu.CompilerParams(
            dimension_semantics=("parallel", "parallel"),
            vmem_limit_bytes=_vmem_limit(rows * d * (2 * 2 * 2 + 4 * 4 + 3 * 2 + 4 * 4) + 4 * d * d)),
        name="fourier_seq_c",
    )(ba.reshape(b, l1, 2 * l2, d), x.reshape(b, l2, l1, d), wc, cs, w_out.astype(BF16), b_out.reshape(1, d))
    return out.reshape(b, l, d)


def _mlstm_mixer(x, norm_g, w_in, b_gate, head_g, w_out):
    heads, dv = head_g.shape
    mv = heads * dv
    q, k, qt, kt, v, gc, gr = _mlstm_proj(x, norm_g, w_in, b_gate, heads=heads, mv=mv)
    hf, hb = _mlstm_scan(q, k, qt, kt, v, gc, gr, heads=heads)
    o_lo = 2 * q.shape[-1] + mv
    return _mlstm_out(hf, hb, x, norm_g, w_in[:, o_lo:o_lo + mv], head_g, w_out)


def _trunk(x, mix_norm_g, m_w_in, m_b_gate, m_head_g, m_w_out, f_w_out, f_b_out,
           ffn_norm_g, ffn_w_up, ffn_conv_w, ffn_conv_b, ffn_w_down, final_norm_g):
    depth = mix_norm_g.shape[0]
    n_mixers = 2
    for i in range(depth):
        j = i // n_mixers
        if i % n_mixers == 0:
            x = _mlstm_mixer(x, mix_norm_g[i], m_w_in[j], m_b_gate[j], m_head_g[j], m_w_out[j])
        else:
            x = _fourier_mixer(x, mix_norm_g[i], f_w_out[j], f_b_out[j])
        x = _conv_ffn(x, ffn_norm_g[i], ffn_w_up[i], ffn_conv_w[i], ffn_conv_b[i], ffn_w_down[i],
                      final_norm_g, final_norm=(i == depth - 1))
    return x


def kernel(x_prompt, x_sample, mix_norm_g, m_w_in, m_b_gate, m_head_g, m_w_out, f_w_out, f_b_out,
           ffn_norm_g, ffn_w_up, ffn_conv_w, ffn_conv_b, ffn_w_down, final_norm_g):
    params = (mix_norm_g, m_w_in, m_b_gate, m_head_g, m_w_out, f_w_out, f_b_out,
              ffn_norm_g, ffn_w_up, ffn_conv_w, ffn_conv_b, ffn_w_down, final_norm_g)
    return (_trunk(x_prompt, *params), _trunk(x_sample, *params))
```

```python
import functools

import numpy as np
import jax
import jax.numpy as jnp
from jax import lax
from jax.experimental import pallas as pl
from jax.experimental.pallas import tpu as pltpu

F32 = jnp.float32
BF16 = jnp.bfloat16

EPS = 1e-6
F_GROUPS = 4
N_GATE_KINDS = 4

V7X_VMEM_BYTES = 64 * 1024 * 1024
V7X_LANES = 128
V7X_SUBLANES = 8
V7X_BF16_ROWS = 16
V7X_MXU_DIM = 256

TOKEN_TILE = 1024
HALO = V7X_SUBLANES
SCAN_CHUNK = 128
N_ROWS = V7X_BF16_ROWS
SCAN_ROWS = 4
DFT_INNER = 128

VMEM_LIMIT_FLOOR = 32 * 1024 * 1024
VMEM_LIMIT_CEIL = V7X_VMEM_BYTES - 8 * 1024 * 1024


def _vmem_limit(nbytes):
    return int(min(max(nbytes, VMEM_LIMIT_FLOOR), VMEM_LIMIT_CEIL))


def _rmsnorm(x, g):
    y = x * lax.rsqrt(jnp.mean(x * x, axis=-1, keepdims=True) + EPS)
    return y * g


def _resident(shape):
    nd = len(shape)
    return pl.BlockSpec(shape, lambda *_: (0,) * nd, pipeline_mode=pl.Buffered(1))


def _ffn_kernel(xm_ref, xp_ref, xn_ref, g_ref, wup_ref, cw_ref, cb_ref, wdn_ref, fg_ref,
                o_ref, h_ref, act_ref, *, tm, n_chunks, fc, conv_w, final_norm):
    i = pl.program_id(1)
    last = pl.num_programs(1) - 1
    g = g_ref[...]
    x = xm_ref[0]
    hp = jnp.where(i > 0, _rmsnorm(xp_ref[0], g), 0.0)
    hn = jnp.where(i < last, _rmsnorm(xn_ref[0], g), 0.0)
    h_ref[0:HALO, :] = hp.astype(BF16)
    h_ref[HALO:HALO + tm, :] = _rmsnorm(x, g).astype(BF16)
    h_ref[HALO + tm:, :] = hn.astype(BF16)
    hb = h_ref[...]
    pad = conv_w // 2
    d_ff = n_chunks * fc

    def conv_cols(lo):
        u = jnp.dot(hb, wup_ref[:, lo:lo + fc], preferred_element_type=F32)
        y = cb_ref[:, lo:lo + fc]
        for j in [pad] + [j for j in range(conv_w) if j != pad]:
            off = HALO - pad + j
            y = y + u[off:off + tm] * cw_ref[j:j + 1, lo:lo + fc]
        return y

    for c in range(n_chunks):
        a = conv_cols(c * fc)
        val = conv_cols(d_ff + c * fc)
        act_ref[:, c * fc:(c + 1) * fc] = (a * jax.nn.sigmoid(a) * val).astype(BF16)
    r = x + jnp.dot(act_ref[...], wdn_ref[...], preferred_element_type=F32)
    if final_norm:
        r = _rmsnorm(r, fg_ref[...])
    o_ref[0] = r


def _conv_ffn(x, norm_g, w_up, conv_w, conv_b, w_down, final_g, *, final_norm):
    b, l, d = x.shape
    d_ff = w_down.shape[0]
    cw = conv_w.shape[0]
    fc = V7X_MXU_DIM
    n_chunks = d_ff // fc
    assert n_chunks * fc == d_ff and cw // 2 <= HALO
    tm = min(TOKEN_TILE, l)
    assert l % tm == 0 and tm % HALO == 0
    nt = l // tm
    rb = tm // HALO
    assert (2 * d_ff) % V7X_LANES == 0 and fc % V7X_LANES == 0
    wup = w_up.astype(BF16)
    cwr = conv_w
    cbr = conv_b.reshape(1, 2 * d_ff)
    wdn = w_down.astype(BF16)
    kern = functools.partial(_ffn_kernel, tm=tm, n_chunks=n_chunks, fc=fc, conv_w=cw, final_norm=final_norm)
    vmem = (wup.size * 2 + wdn.size * 2 + 4 * tm * d * 4 + (tm + 2 * HALO) * d * 2 + tm * d_ff * 2
            + 8 * (tm + 2 * HALO) * fc * 4 + 3 * tm * d * 4)
    return pl.pallas_call(
        kern,
        grid=(b, nt),
        in_specs=[
            pl.BlockSpec((1, tm, d), lambda bi, i: (bi, i, 0)),
            pl.BlockSpec((1, HALO, d), lambda bi, i: (bi, jnp.maximum(i * rb - 1, 0), 0)),
            pl.BlockSpec((1, HALO, d), lambda bi, i: (bi, jnp.minimum((i + 1) * rb, l // HALO - 1), 0)),
            _resident((1, d)),
            _resident(wup.shape),
            _resident(cwr.shape),
            _resident(cbr.shape),
            _resident(wdn.shape),
            _resident((1, d)),
        ],
        out_specs=pl.BlockSpec((1, tm, d), lambda bi, i: (bi, i, 0)),
        out_shape=jax.ShapeDtypeStruct((b, l, d), F32),
        scratch_shapes=[pltpu.VMEM((tm + 2 * HALO, d), BF16), pltpu.VMEM((tm, d_ff), BF16)],
        compiler_params=pltpu.CompilerParams(
            dimension_semantics=("parallel", "arbitrary"), vmem_limit_bytes=_vmem_limit(vmem)),
        name="conv_ffn",
    )(x, x, x, norm_g.reshape(1, d), wup, cwr, cbr, wdn, final_g.reshape(1, d))


def _split3(x):
    hi = x.astype(BF16)
    r = x - hi.astype(F32)
    mid = r.astype(BF16)
    lo = (r - mid.astype(F32)).astype(BF16)
    return hi, mid, lo


def _proj_kernel(x_ref, g_ref, w_ref, wg_ref, bg_ref, pre_ref,
                 q_ref, k_ref, qt_ref, kt_ref, v_ref, gc_ref, gr_ref, *, mq, mv, heads, k_scale):
    h = _rmsnorm(x_ref[0], g_ref[...]).astype(BF16)
    gt = jnp.dot(h, wg_ref[...], preferred_element_type=F32) + bg_ref[...]
    q = jnp.dot(h, w_ref[:, 0:mq], preferred_element_type=F32)
    q_ref[0] = q.astype(BF16)
    qt_ref[0] = q.T.astype(BF16)
    lane = lax.broadcasted_iota(jnp.int32, gt.shape, 1)
    kind = lane // heads
    log_sig = jnp.minimum(gt, 0.0) - jnp.log1p(jnp.exp(-jnp.abs(gt)))
    parts = _split3(log_sig)
    t = pre_ref.shape[0]
    pre, suf = [], []
    for c in range(gt.shape[0] // t):
        rows = slice(c * t, (c + 1) * t)
        p = sum(jnp.dot(pre_ref[...], part[rows], preferred_element_type=F32) for part in parts)
        pre.append(p)
        suf.append(p[t - 1:t, :] - p + log_sig[rows])
    k = jnp.dot(h, w_ref[:, mq:2 * mq], preferred_element_type=F32) * k_scale
    k_ref[0] = k.astype(BF16)
    kt_ref[0] = k.T.astype(BF16)
    gt = jnp.where(kind == 1, jnp.concatenate(pre, axis=0), jnp.where(kind == 3, jnp.concatenate(suf, axis=0), gt))
    gc_ref[0] = gt
    gr_ref[0] = gt.T[0:gr_ref.shape[1], :]
    v_ref[0] = jnp.dot(h, w_ref[:, 2 * mq:2 * mq + mv], preferred_element_type=F32).astype(BF16)


def _mlstm_proj(x, norm_g, w_in, b_gate, *, heads, mv):
    b, l, d = x.shape
    ng = N_GATE_KINDS * heads
    mq = (w_in.shape[1] - 2 * mv - ng) // 2
    dqk = mq // heads
    tm = min(TOKEN_TILE, l)
    assert l % tm == 0 and ng <= V7X_LANES
    w_main = w_in[:, :2 * mq + mv].astype(BF16)
    w_gate = jnp.pad(w_in[:, 2 * mq + 2 * mv:], ((0, 0), (0, V7X_LANES - ng))).astype(BF16)
    b_pad = jnp.pad(b_gate, (0, V7X_LANES - ng)).reshape(1, V7X_LANES)
    t = min(SCAN_CHUNK, l)
    assert tm % t == 0
    pos = np.arange(t)
    pre_m = jnp.asarray(pos[None, :] <= pos[:, None], BF16)
    kern = functools.partial(_proj_kernel, mq=mq, mv=mv, heads=heads, k_scale=float(dqk) ** -0.5)
    row = lambda bi, i: (bi, i, 0)
    col = lambda bi, i: (bi, 0, i)
    vmem = (w_main.size * 2 + w_gate.size * 2 + 2 * tm * d * 4 + 2 * tm * (4 * mq + mv) * 2 + 8 * tm * mv * 4)
    return pl.pallas_call(
        kern,
        grid=(b, l // tm),
        in_specs=[
            pl.BlockSpec((1, tm, d), row),
            _resident((1, d)),
            _resident(w_main.shape),
            _resident(w_gate.shape),
            _resident((1, V7X_LANES)),
            _resident((t, t)),
        ],
        out_specs=[
            pl.BlockSpec((1, tm, mq), row),
            pl.BlockSpec((1, tm, mq), row),
            pl.BlockSpec((1, mq, tm), col),
            pl.BlockSpec((1, mq, tm), col),
            pl.BlockSpec((1, tm, mv), row),
            pl.BlockSpec((1, tm, V7X_LANES), row),
            pl.BlockSpec((1, ng, tm), col),
        ],
        out_shape=[
            jax.ShapeDtypeStruct((b, l, mq), BF16),
            jax.ShapeDtypeStruct((b, l, mq), BF16),
            jax.ShapeDtypeStruct((b, mq, l), BF16),
            jax.ShapeDtypeStruct((b, mq, l), BF16),
            jax.ShapeDtypeStruct((b, l, mv), BF16),
            jax.ShapeDtypeStruct((b, l, V7X_LANES), F32),
            jax.ShapeDtypeStruct((b, ng, l), F32),
        ],
        compiler_params=pltpu.CompilerParams(
            dimension_semantics=("parallel", "parallel"), vmem_limit_bytes=_vmem_limit(vmem)),
        name="mlstm_proj",
    )(x, norm_g.reshape(1, d), w_main, w_gate, b_pad, pre_m)


def _scan_kernel(qf_ref, kf_ref, qtf_ref, ktf_ref, vf_ref, gcf_ref, grf_ref,
                 qb_ref, kb_ref, qtb_ref, ktb_ref, vb_ref, gcb_ref, grb_ref,
                 hf_ref, hb_ref, c_ref, rhs_ref, n_ref, m_ref, *, t, heads, dqk, dv):
    step = pl.program_id(1)
    par = step % 2

    @pl.when(step == 0)
    def _():
        c_ref[...] = jnp.zeros_like(c_ref)
        rhs_ref[...] = jnp.zeros_like(rhs_ref)
        n_ref[...] = jnp.zeros_like(n_ref)
        m_ref[...] = jnp.zeros_like(m_ref)

    src_i = lax.broadcasted_iota(jnp.int32, (t, t), 0)
    tgt_i = lax.broadcasted_iota(jnp.int32, (t, t), 1)
    first_head = lax.broadcasted_iota(jnp.int32, (t, 2 * dqk), 1) < dqk
    n_row = lax.broadcasted_iota(jnp.int32, (N_ROWS, 2 * t), 0)
    n_lane_first = lax.broadcasted_iota(jnp.int32, (N_ROWS, 2 * t), 1) < t
    own_block = ((n_row == 0) & n_lane_first) | ((n_row == 1) & jnp.logical_not(n_lane_first))
    n_row_q = lax.broadcasted_iota(jnp.int32, (N_ROWS, 2 * dqk), 0)
    fwd_refs = (qf_ref, kf_ref, qtf_ref, ktf_ref, vf_ref, gcf_ref, grf_ref, hf_ref)
    bwd_refs = (qb_ref, kb_ref, qtb_ref, ktb_ref, vb_ref, gcb_ref, grb_ref, hb_ref)
    streams = []
    for bb in range(hf_ref.shape[0]):
        streams.append(tuple(r.at[bb:bb + 1] for r in fwd_refs) + (src_i <= tgt_i, 0, t - 1))
        streams.append(tuple(r.at[bb:bb + 1] for r in bwd_refs) + (src_i >= tgt_i, 2 * heads, 0))
    dn_nt = (((1,), (1,)), ((), ()))
    dn_tn = (((0,), (0,)), ((), ()))
    half = heads // 2

    def block_diag(a2):
        zero = jnp.zeros_like(a2)
        return jnp.concatenate([jnp.where(first_head, a2, zero), jnp.where(first_head, zero, a2)], axis=0)

    every = [(di, h, di * heads + h) for di in range(len(streams)) for h in range(heads)]
    pairs = [(di, p, di * half + p) for di in range(len(streams)) for p in range(half)]
    st = {}
    for di, h, idx in every:
        gc_ref, gr_ref, base, end = streams[di][5], streams[di][6], streams[di][9], streams[di][10]
        ig_r = gr_ref[0, base + h:base + h + 1, :]
        b_r = gr_ref[0, base + heads + h:base + heads + h + 1, :]
        total = b_r[:, end:end + 1]
        m_prev = m_ref[idx]
        st[idx] = dict(b_r=b_r, inter=b_r + m_prev, gk=total - b_r + ig_r, carry=total + m_prev,
                       src_c=gc_ref[0, :, base + h:base + h + 1] - gc_ref[0, :, base + heads + h:base + heads + h + 1])
    for di, h, idx in every:
        e = st[idx]
        m_new = jnp.maximum(e["carry"], jnp.max(e["gk"], axis=-1, keepdims=True))
        m_ref[idx] = m_new
        e["wk"] = jnp.exp(e["gk"] - m_new)
        e["decay"] = jnp.exp(e["carry"] - m_new)

    sp = {}
    for di, p, pi in pairs:
        sl = slice(2 * p * dqk, (2 * p + 2) * dqk)
        sp[pi] = dict(k2=streams[di][1][0, :, sl], n_prev=n_ref[pi])
        lhs = jnp.concatenate([sp[pi]["k2"], sp[pi]["n_prev"].astype(BF16)], axis=0)
        s2 = lax.dot_general(lhs, block_diag(streams[di][0][0, :, sl]), dn_nt, preferred_element_type=F32)
        for j in range(2):
            e = st[di * heads + 2 * p + j]
            e["qk"] = s2[:t, j * t:(j + 1) * t]
            e["qn"] = s2[t + j:t + j + 1, j * t:(j + 1) * t]

    for di, h, idx in every:
        e = st[idx]
        e["v"] = streams[di][4][0, :, h * dv:(h + 1) * dv]
        e["kwt"] = (streams[di][3][0, h * dqk:(h + 1) * dqk, :].astype(F32) * e["wk"]).astype(BF16)
    for di, h, idx in every:
        e = st[idx]
        e["upd"] = jnp.dot(e["kwt"], e["v"], preferred_element_type=F32)
        rhs_ref[par, idx, :t, :] = e["v"]
    for di, p, pi in pairs:
        ea, eb = st[di * heads + 2 * p], st[di * heads + 2 * p + 1]
        wk2 = jnp.concatenate([ea["wk"], eb["wk"]], axis=1)
        w_rows = jnp.where(own_block, wk2, 0.0).astype(BF16)
        dec = jnp.where(n_row_q == 0, ea["decay"], eb["decay"])
        n_ref[pi] = dec * sp[pi]["n_prev"] + jnp.dot(w_rows, block_diag(sp[pi]["k2"]), preferred_element_type=F32)
    for di, h, idx in every:
        e = st[idx]
        c_new = e["decay"] * c_ref[idx] + e["upd"]
        c_ref[idx] = c_new
        rhs_ref[1 - par, idx, t:, :] = c_new.astype(BF16)

    for di, h, idx in every:
        e = st[idx]
        e["dlog"] = jnp.where(streams[di][8], e["src_c"] + e["b_r"], -jnp.inf)
    for di, h, idx in every:
        e = st[idx]
        e["m_t"] = jnp.maximum(e["inter"], jnp.max(e["dlog"], axis=0, keepdims=True))
    for di, h, idx in every:
        e = st[idx]
        e["s"] = e["qk"] * jnp.exp(e["dlog"] - e["m_t"])
        e["sc"] = jnp.exp(e["inter"] - e["m_t"])
    for di, h, idx in every:
        e = st[idx]
        den = jnp.sum(e["s"], axis=0, keepdims=True) + e["sc"] * e["qn"]
        e["r"] = 1.0 / jnp.maximum(jnp.abs(den), jnp.exp(-e["m_t"]))
    for di, h, idx in every:
        e = st[idx]
        qt32 = streams[di][2][0, h * dqk:(h + 1) * dqk, :].astype(F32)
        e["lhs_t"] = jnp.concatenate(
            [(e["s"] * e["r"]).astype(BF16), (qt32 * (e["sc"] * e["r"])).astype(BF16)], axis=0)
    for di, h, idx in every:
        streams[di][7][0, :, h * dv:(h + 1) * dv] = lax.dot_general(
            st[idx]["lhs_t"], rhs_ref[par, idx], dn_tn, preferred_element_type=F32).astype(streams[di][7].dtype)


def _mlstm_scan(q, k, qt, kt, v, gc, gr, *, heads):
    b, l, mq = q.shape
    mv = v.shape[-1]
    dqk, dv = mq // heads, mv // heads
    t = min(SCAN_CHUNK, l)
    assert l % t == 0 and heads % 2 == 0
    nc = l // t
    ng = gr.shape[1]
    kern = functools.partial(_scan_kernel, t=t, heads=heads, dqk=dqk, dv=dv)
    fwd = lambda bi, c: (bi, c, 0)
    bwd = lambda bi, c: (bi, nc - 1 - c, 0)
    fwd_r = lambda bi, c: (bi, 0, c)
    bwd_r = lambda bi, c: (bi, 0, nc - 1 - c)

    rows = SCAN_ROWS if b % SCAN_ROWS == 0 else 1
    chains = 2 * rows * heads

    def stream(tok, rowm):
        return [pl.BlockSpec((rows, t, mq), tok), pl.BlockSpec((rows, t, mq), tok),
                pl.BlockSpec((rows, mq, t), rowm), pl.BlockSpec((rows, mq, t), rowm),
                pl.BlockSpec((rows, t, mv), tok),
                pl.BlockSpec((rows, t, V7X_LANES), tok), pl.BlockSpec((rows, ng, t), rowm)]

    return pl.pallas_call(
        kern,
        grid=(b // rows, nc),
        in_specs=stream(fwd, fwd_r) + stream(bwd, bwd_r),
        out_specs=[pl.BlockSpec((rows, t, mv), fwd), pl.BlockSpec((rows, t, mv), bwd)],
        out_shape=[jax.ShapeDtypeStruct((b, l, mv), BF16), jax.ShapeDtypeStruct((b, l, mv), BF16)],
        scratch_shapes=[pltpu.VMEM((chains, dqk, dv), F32), pltpu.VMEM((2, chains, t + dqk, dv), BF16),
                        pltpu.VMEM((chains // 2, N_ROWS, 2 * dqk), F32), pltpu.VMEM((chains, 1, 1), F32)],
        compiler_params=pltpu.CompilerParams(dimension_semantics=("parallel", "arbitrary")),
        name="mlstm_scan",
    )(q, k, qt, kt, v, gc, gr, q, k, qt, kt, v, gc, gr)


def _mout_kernel(hf_ref, hb_ref, x_ref, g_ref, wo_ref, hg_ref, w_ref, out_ref, y_ref, *, heads, dv):
    tm = x_ref.shape[1]
    parts = 4 if tm % (4 * V7X_MXU_DIM) == 0 else 1
    halves = [slice(r, r + tm // parts) for r in range(0, tm, tm // parts)]
    gate = [jnp.dot(_rmsnorm(x_ref[0, rows, :], g_ref[...]).astype(BF16), wo_ref[...], preferred_element_type=F32)
            for rows in halves]
    for rows, o in zip(halves, gate):
        for h in range(heads):
            sl = slice(h * dv, (h + 1) * dv)
            hs = hf_ref[0, rows, sl].astype(F32) + hb_ref[0, rows, sl].astype(F32)
            hs = hs * lax.rsqrt(jnp.mean(hs * hs, axis=-1, keepdims=True) + EPS)
            hs = hs * hg_ref[:, sl]
            y_ref[rows, sl] = (hs * jax.nn.sigmoid(o[:, sl])).astype(BF16)
        out_ref[0, rows, :] = x_ref[0, rows, :] + jnp.dot(y_ref[rows, :], w_ref[...], preferred_element_type=F32)


def _mlstm_out(hf, hb, x, norm_g, w_gate_out, head_g, w_out):
    b, l, d = x.shape
    heads, dv = head_g.shape
    mv = heads * dv
    tm = min(TOKEN_TILE, l)
    assert l % tm == 0
    row = lambda bi, i: (bi, i, 0)
    w = w_out.astype(BF16)
    wo = w_gate_out.astype(BF16)
    vmem = (w.size + wo.size) * 2 + 2 * tm * (2 * mv * 2 + 2 * d * 4) + tm * mv * 2 + 8 * tm * d * 4
    return pl.pallas_call(
        functools.partial(_mout_kernel, heads=heads, dv=dv),
        grid=(b, l // tm),
        in_specs=[pl.BlockSpec((1, tm, mv), row), pl.BlockSpec((1, tm, mv), row), pl.BlockSpec((1, tm, d), row),
                  _resident((1, d)), _resident(wo.shape), _resident((1, mv)), _resident(w.shape)],
        out_specs=pl.BlockSpec((1, tm, d), row),
        out_shape=jax.ShapeDtypeStruct((b, l, d), F32),
        scratch_shapes=[pltpu.VMEM((tm, mv), BF16)],
        compiler_params=pltpu.CompilerParams(
            dimension_semantics=("parallel", "parallel"), vmem_limit_bytes=_vmem_limit(vmem)),
        name="mlstm_out",
    )(hf, hb, x, norm_g.reshape(1, d), wo, head_g.reshape(1, mv), w)


def _dft_tables(l1, l2, gd, n2h, n2b):
    l = l1 * l2
    a1 = 2.0 * np.pi * np.outer(np.arange(l1), np.arange(l1)) / l1
    wa = np.stack([np.cos(a1), -np.sin(a1)]) / np.sqrt(l1)
    wa = np.einsum("rkn,ij->krinj", wa, np.eye(n2h)).reshape(l1 * 2 * n2h, l1 * n2h)
    tw = 2.0 * np.pi * np.einsum("k,jhi->jhki", np.arange(l1),
                                 np.arange(l2).reshape(l2 // n2b, n2b // n2h, n2h)) / l
    twr, twi = np.cos(tw)[..., None], -np.sin(tw)[..., None]
    a2 = 2.0 * np.pi * np.outer(np.arange(l2), np.arange(l2)) / l2
    c2, s2 = np.cos(a2), np.sin(a2)
    wc = np.block([[c2, s2], [-s2, c2]]) / np.sqrt(l2)
    ac = 2.0 * np.pi * np.outer(np.arange(gd), np.arange(gd)) / gd
    cs = np.stack([np.cos(ac), np.sin(ac)]) / np.sqrt(gd)
    return (jnp.asarray(wa, BF16), jnp.asarray(twr, F32), jnp.asarray(twi, F32),
            jnp.asarray(wc, BF16), jnp.asarray(cs, BF16))


def _fa_kernel(x_ref, g_ref, wa_ref, twr_ref, twi_ref, out_ref, *, l1, n2h, n2b, d):
    g = g_ref[...]
    wa = wa_ref[...]
    halves = []
    for hh in range(n2b // n2h):
        x = x_ref[0, :, hh * n2h:(hh + 1) * n2h, :].reshape(l1 * n2h, d)
        a = jnp.dot(wa, _rmsnorm(x, g).astype(BF16), preferred_element_type=F32)
        a = a.reshape(l1, 2, n2h, d)
        ar = a[:, 0]
        ai = a[:, 1]
        tr = twr_ref[0, hh]
        ti = twi_ref[0, hh]
        halves.append(jnp.stack([ar * tr - ai * ti, ar * ti + ai * tr], axis=1))
    out_ref[0] = jnp.concatenate(halves, axis=2).astype(BF16)


def _fc_kernel(ba_ref, x_ref, wc_ref, cs_ref, wout_ref, bout_ref, out_ref, z_ref, f_ref, *, l2, k1b, d, gd):
    wc = wc_ref[...]
    for j in range(k1b):
        z = jnp.dot(wc, ba_ref[0, j], preferred_element_type=F32)
        z_ref[0, j * l2:(j + 1) * l2, :] = z[:l2].astype(BF16)
        z_ref[1, j * l2:(j + 1) * l2, :] = z[l2:].astype(BF16)
    for gi in range(d // gd):
        sl = slice(gi * gd, (gi + 1) * gd)
        f = (jnp.dot(z_ref[0, :, sl], cs_ref[0], preferred_element_type=F32)
             + jnp.dot(z_ref[1, :, sl], cs_ref[1], preferred_element_type=F32))
        f_ref[:, sl] = f.astype(BF16)
    y = jnp.dot(f_ref[...], wout_ref[...], preferred_element_type=F32) + bout_ref[...]
    out_ref[0] = x_ref[0] + pltpu.einshape("jkd->kjd", y.reshape(k1b, l2, d))


def _fourier_mixer(x, norm_g, w_out, b_out):
    b, l, d = x.shape
    l2 = min(DFT_INNER, l)
    l1 = l // l2
    assert l1 * l2 == l and d % F_GROUPS == 0
    gd = d // F_GROUPS
    n2h = V7X_SUBLANES
    n2b = min(V7X_BF16_ROWS, l2)
    k1b = min(V7X_SUBLANES, l1)
    assert l2 % n2b == 0 and n2b % n2h == 0 and l1 % k1b == 0
    wa, twr, twi, wc, cs = _dft_tables(l1, l2, gd, n2h, n2b)

    tw_spec = pl.BlockSpec((1, n2b // n2h, l1, n2h, 1), lambda bi, j: (j, 0, 0, 0, 0))
    ba = pl.pallas_call(
        functools.partial(_fa_kernel, l1=l1, n2h=n2h, n2b=n2b, d=d),
        grid=(b, l2 // n2b),
        in_specs=[pl.BlockSpec((1, l1, n2b, d), lambda bi, j: (bi, 0, j, 0)),
                  _resident((1, d)), _resident(wa.shape), tw_spec, tw_spec],
        out_specs=pl.BlockSpec((1, l1, 2, n2b, d), lambda bi, j: (bi, 0, 0, j, 0)),
        out_shape=jax.ShapeDtypeStruct((b, l1, 2, l2, d), BF16),
        compiler_params=pltpu.CompilerParams(
            dimension_semantics=("parallel", "parallel"),
            vmem_limit_bytes=_vmem_limit(l1 * n2b * d * (2 * 4 + 2 * 2 * 2 + 6 * 4) + wa.size * 2)),
        name="fourier_seq_a",
    )(x.reshape(b, l1, l2, d), norm_g.reshape(1, d), wa, twr, twi)

    rows = k1b * l2
    out = pl.pallas_call(
        functools.partial(_fc_kernel, l2=l2, k1b=k1b, d=d, gd=gd),
        grid=(b, l1 // k1b),
        in_specs=[pl.BlockSpec((1, k1b, 2 * l2, d), lambda bi, j: (bi, j, 0, 0)),
                  pl.BlockSpec((1, l2, k1b, d), lambda bi, j: (bi, 0, j, 0)),
                  _resident(wc.shape), _resident(cs.shape), _resident((d, d)), _resident((1, d))],
        out_specs=pl.BlockSpec((1, l2, k1b, d), lambda bi, j: (bi, 0, j, 0)),
        out_shape=jax.ShapeDtypeStruct((b, l2, l1, d), F32),
        scratch_shapes=[pltpu.VMEM((2, rows, d), BF16), pltpu.VMEM((rows, d), BF16)],
        compiler_params=pltpu.CompilerParams(
            dimension_semantics=("parallel", "parallel"),
            vmem_limit_bytes=_vmem_limit(rows * d * (2 * 2 * 2 + 4 * 4 + 3 * 2 + 4 * 4) + 4 * d * d)),
        name="fourier_seq_c",
    )(ba.reshape(b, l1, 2 * l2, d), x.reshape(b, l2, l1, d), wc, cs, w_out.astype(BF16), b_out.reshape(1, d))
    return out.reshape(b, l, d)


def _mlstm_mixer(x, norm_g, w_in, b_gate, head_g, w_out):
    heads, dv = head_g.shape
    mv = heads * dv
    q, k, qt, kt, v, gc, gr = _mlstm_proj(x, norm_g, w_in, b_gate, heads=heads, mv=mv)
    hf, hb = _mlstm_scan(q, k, qt, kt, v, gc, gr, heads=heads)
    o_lo = 2 * q.shape[-1] + mv
    return _mlstm_out(hf, hb, x, norm_g, w_in[:, o_lo:o_lo + mv], head_g, w_out)


def _trunk(x, mix_norm_g, m_w_in, m_b_gate, m_head_g, m_w_out, f_w_out, f_b_out,
           ffn_norm_g, ffn_w_up, ffn_conv_w, ffn_conv_b, ffn_w_down, final_norm_g):
    depth = mix_norm_g.shape[0]
    n_mixers = 2
    for i in range(depth):
        j = i // n_mixers
        if i % n_mixers == 0:
            x = _mlstm_mixer(x, mix_norm_g[i], m_w_in[j], m_b_gate[j], m_head_g[j], m_w_out[j])
        else:
            x = _fourier_mixer(x, mix_norm_g[i], f_w_out[j], f_b_out[j])
        x = _conv_ffn(x, ffn_norm_g[i], ffn_w_up[i], ffn_conv_w[i], ffn_conv_b[i], ffn_w_down[i],
                      final_norm_g, final_norm=(i == depth - 1))
    return x


def kernel(x_prompt, x_sample, mix_norm_g, m_w_in, m_b_gate, m_head_g, m_w_out, f_w_out, f_b_out,
           ffn_norm_g, ffn_w_up, ffn_conv_w, ffn_conv_b, ffn_w_down, final_norm_g):
    params = (mix_norm_g, m_w_in, m_b_gate, m_head_g, m_w_out, f_w_out, f_b_out,
              ffn_norm_g, ffn_w_up, ffn_conv_w, ffn_conv_b, ffn_w_down, final_norm_g)
    return (_trunk(x_prompt, *params), _trunk(x_sample, *params))
```

```python
import functools

import numpy as np
import jax
import jax.numpy as jnp
from jax import lax
from jax.experimental import pallas as pl
from jax.experimental.pallas import tpu as pltpu

F32 = jnp.float32
BF16 = jnp.bfloat16

EPS = 1e-6
F_GROUPS = 4
N_GATE_KINDS = 4

V7X_VMEM_BYTES = 64 * 1024 * 1024
V7X_LANES = 128
V7X_SUBLANES = 8
V7X_BF16_ROWS = 16
V7X_MXU_DIM = 256

TOKEN_TILE = 1024
HALO = V7X_SUBLANES
SCAN_CHUNK = 128
N_ROWS = V7X_BF16_ROWS
SCAN_ROWS = 4
DFT_INNER = 128

VMEM_LIMIT_FLOOR = 32 * 1024 * 1024
VMEM_LIMIT_CEIL = V7X_VMEM_BYTES - 8 * 1024 * 1024


def _vmem_limit(nbytes):
    return int(min(max(nbytes, VMEM_LIMIT_FLOOR), VMEM_LIMIT_CEIL))


def _rmsnorm(x, g):
    y = x * lax.rsqrt(jnp.mean(x * x, axis=-1, keepdims=True) + EPS)
    return y * g


def _resident(shape):
    nd = len(shape)
    return pl.BlockSpec(shape, lambda *_: (0,) * nd, pipeline_mode=pl.Buffered(1))


def _ffn_kernel(xm_ref, xp_ref, xn_ref, g_ref, wup_ref, cw_ref, cb_ref, wdn_ref, fg_ref,
                o_ref, h_ref, act_ref, *, tm, n_chunks, fc, conv_w, final_norm):
    i = pl.program_id(1)
    last = pl.num_programs(1) - 1
    g = g_ref[...]
    x = xm_ref[0]
    hp = jnp.where(i > 0, _rmsnorm(xp_ref[0], g), 0.0)
    hn = jnp.where(i < last, _rmsnorm(xn_ref[0], g), 0.0)
    h_ref[0:HALO, :] = hp.astype(BF16)
    h_ref[HALO:HALO + tm, :] = _rmsnorm(x, g).astype(BF16)
    h_ref[HALO + tm:, :] = hn.astype(BF16)
    hb = h_ref[...]
    pad = conv_w // 2
    d_ff = n_chunks * fc

    def conv_cols(lo):
        u = jnp.dot(hb, wup_ref[:, lo:lo + fc], preferred_element_type=F32)
        y = cb_ref[:, lo:lo + fc]
        for j in [pad] + [j for j in range(conv_w) if j != pad]:
            off = HALO - pad + j
            y = y + u[off:off + tm] * cw_ref[j:j + 1, lo:lo + fc]
        return y

    for c in range(n_chunks):
        a = conv_cols(c * fc)
        val = conv_cols(d_ff + c * fc)
        act_ref[:, c * fc:(c + 1) * fc] = (a * jax.nn.sigmoid(a) * val).astype(BF16)
    r = x + jnp.dot(act_ref[...], wdn_ref[...], preferred_element_type=F32)
    if final_norm:
        r = _rmsnorm(r, fg_ref[...])
    o_ref[0] = r


def _conv_ffn(x, norm_g, w_up, conv_w, conv_b, w_down, final_g, *, final_norm):
    b, l, d = x.shape
    d_ff = w_down.shape[0]
    cw = conv_w.shape[0]
    fc = V7X_MXU_DIM
    n_chunks = d_ff // fc
    assert n_chunks * fc == d_ff and cw // 2 <= HALO
    tm = min(TOKEN_TILE, l)
    assert l % tm == 0 and tm % HALO == 0
    nt = l // tm
    rb = tm // HALO
    assert (2 * d_ff) % V7X_LANES == 0 and fc % V7X_LANES == 0
    wup = w_up.astype(BF16)
    cwr = conv_w
    cbr = conv_b.reshape(1, 2 * d_ff)
    wdn = w_down.astype(BF16)
    kern = functools.partial(_ffn_kernel, tm=tm, n_chunks=n_chunks, fc=fc, conv_w=cw, final_norm=final_norm)
    vmem = (wup.size * 2 + wdn.size * 2 + 4 * tm * d * 4 + (tm + 2 * HALO) * d * 2 + tm * d_ff * 2
            + 8 * (tm + 2 * HALO) * fc * 4 + 3 * tm * d * 4)
    return pl.pallas_call(
        kern,
        grid=(b, nt),
        in_specs=[
            pl.BlockSpec((1, tm, d), lambda bi, i: (bi, i, 0)),
            pl.BlockSpec((1, HALO, d), lambda bi, i: (bi, jnp.maximum(i * rb - 1, 0), 0)),
            pl.BlockSpec((1, HALO, d), lambda bi, i: (bi, jnp.minimum((i + 1) * rb, l // HALO - 1), 0)),
            _resident((1, d)),
            _resident(wup.shape),
            _resident(cwr.shape),
            _resident(cbr.shape),
            _resident(wdn.shape),
            _resident((1, d)),
        ],
        out_specs=pl.BlockSpec((1, tm, d), lambda bi, i: (bi, i, 0)),
        out_shape=jax.ShapeDtypeStruct((b, l, d), F32),
        scratch_shapes=[pltpu.VMEM((tm + 2 * HALO, d), BF16), pltpu.VMEM((tm, d_ff), BF16)],
        compiler_params=pltpu.CompilerParams(
            dimension_semantics=("parallel", "arbitrary"), vmem_limit_bytes=_vmem_limit(vmem)),
        name="conv_ffn",
    )(x, x, x, norm_g.reshape(1, d), wup, cwr, cbr, wdn, final_g.reshape(1, d))


def _split3(x):
    hi = x.astype(BF16)
    r = x - hi.astype(F32)
    mid = r.astype(BF16)
    lo = (r - mid.astype(F32)).astype(BF16)
    return hi, mid, lo


def _proj_kernel(x_ref, g_ref, w_ref, wg_ref, bg_ref, pre_ref,
                 q_ref, k_ref, qt_ref, kt_ref, v_ref, gc_ref, gr_ref, *, mq, mv, heads, k_scale):
    h = _rmsnorm(x_ref[0], g_ref[...]).astype(BF16)
    gt = jnp.dot(h, wg_ref[...], preferred_element_type=F32) + bg_ref[...]
    q = jnp.dot(h, w_ref[:, 0:mq], preferred_element_type=F32)
    q_ref[0] = q.astype(BF16)
    qt_ref[0] = q.T.astype(BF16)
    lane = lax.broadcasted_iota(jnp.int32, gt.shape, 1)
    kind = lane // heads
    log_sig = jnp.minimum(gt, 0.0) - jnp.log1p(jnp.exp(-jnp.abs(gt)))
    parts = _split3(log_sig)
    t = pre_ref.shape[0]
    pre, suf = [], []
    for c in range(gt.shape[0] // t):
        rows = slice(c * t, (c + 1) * t)
        p = sum(jnp.dot(pre_ref[...], part[rows], preferred_element_type=F32) for part in parts)
        pre.append(p)
        suf.append(p[t - 1:t, :] - p + log_sig[rows])
    k = jnp.dot(h, w_ref[:, mq:2 * mq], preferred_element_type=F32) * k_scale
    k_ref[0] = k.astype(BF16)
    kt_ref[0] = k.T.astype(BF16)
    gt = jnp.where(kind == 1, jnp.concatenate(pre, axis=0), jnp.where(kind == 3, jnp.concatenate(suf, axis=0), gt))
    gc_ref[0] = gt
    gr_ref[0] = gt.T[0:gr_ref.shape[1], :]
    v_ref[0] = jnp.dot(h, w_ref[:, 2 * mq:2 * mq + mv], preferred_element_type=F32).astype(BF16)


def _mlstm_proj(x, norm_g, w_in, b_gate, *, heads, mv):
    b, l, d = x.shape
    ng = N_GATE_KINDS * heads
    mq = (w_in.shape[1] - 2 * mv - ng) // 2
    dqk = mq // heads
    tm = min(TOKEN_TILE, l)
    assert l % tm == 0 and ng <= V7X_LANES
    w_main = w_in[:, :2 * mq + mv].astype(BF16)
    w_gate = jnp.pad(w_in[:, 2 * mq + 2 * mv:], ((0, 0), (0, V7X_LANES - ng))).astype(BF16)
    b_pad = jnp.pad(b_gate, (0, V7X_LANES - ng)).reshape(1, V7X_LANES)
    t = min(SCAN_CHUNK, l)
    assert tm % t == 0
    pos = np.arange(t)
    pre_m = jnp.asarray(pos[None, :] <= pos[:, None], BF16)
    kern = functools.partial(_proj_kernel, mq=mq, mv=mv, heads=heads, k_scale=float(dqk) ** -0.5)
    row = lambda bi, i: (bi, i, 0)
    col = lambda bi, i: (bi, 0, i)
    vmem = (w_main.size * 2 + w_gate.size * 2 + 2 * tm * d * 4 + 2 * tm * (4 * mq + mv) * 2 + 8 * tm * mv * 4)
    return pl.pallas_call(
        kern,
        grid=(b, l // tm),
        in_specs=[
            pl.BlockSpec((1, tm, d), row),
            _resident((1, d)),
            _resident(w_main.shape),
            _resident(w_gate.shape),
            _resident((1, V7X_LANES)),
            _resident((t, t)),
        ],
        out_specs=[
            pl.BlockSpec((1, tm, mq), row),
            pl.BlockSpec((1, tm, mq), row),
            pl.BlockSpec((1, mq, tm), col),
            pl.BlockSpec((1, mq, tm), col),
            pl.BlockSpec((1, tm, mv), row),
            pl.BlockSpec((1, tm, V7X_LANES), row),
            pl.BlockSpec((1, ng, tm), col),
        ],
        out_shape=[
            jax.ShapeDtypeStruct((b, l, mq), BF16),
            jax.ShapeDtypeStruct((b, l, mq), BF16),
            jax.ShapeDtypeStruct((b, mq, l), BF16),
            jax.ShapeDtypeStruct((b, mq, l), BF16),
            jax.ShapeDtypeStruct((b, l, mv), BF16),
            jax.ShapeDtypeStruct((b, l, V7X_LANES), F32),
            jax.ShapeDtypeStruct((b, ng, l), F32),
        ],
        compiler_params=pltpu.CompilerParams(
            dimension_semantics=("parallel", "parallel"), vmem_limit_bytes=_vmem_limit(vmem)),
        name="mlstm_proj",
    )(x, norm_g.reshape(1, d), w_main, w_gate, b_pad, pre_m)


def _scan_kernel(qf_ref, kf_ref, qtf_ref, ktf_ref, vf_ref, gcf_ref, grf_ref,
                 qb_ref, kb_ref, qtb_ref, ktb_ref, vb_ref, gcb_ref, grb_ref,
                 hf_ref, hb_ref, c_ref, rhs_ref, n_ref, m_ref, *, t, heads, dqk, dv):
    step = pl.program_id(1)
    par = step % 2

    @pl.when(step == 0)
    def _():
        c_ref[...] = jnp.zeros_like(c_ref)
        rhs_ref[...] = jnp.zeros_like(rhs_ref)
        n_ref[...] = jnp.zeros_like(n_ref)
        m_ref[...] = jnp.zeros_like(m_ref)

    src_i = lax.broadcasted_iota(jnp.int32, (t, t), 0)
    tgt_i = lax.broadcasted_iota(jnp.int32, (t, t), 1)
    first_head = lax.broadcasted_iota(jnp.int32, (t, 2 * dqk), 1) < dqk
    n_row = lax.broadcasted_iota(jnp.int32, (N_ROWS, 2 * t), 0)
    n_lane_first = lax.broadcasted_iota(jnp.int32, (N_ROWS, 2 * t), 1) < t
    own_block = ((n_row == 0) & n_lane_first) | ((n_row == 1) & jnp.logical_not(n_lane_first))
    n_row_q = lax.broadcasted_iota(jnp.int32, (N_ROWS, 2 * dqk), 0)
    fwd_refs = (qf_ref, kf_ref, qtf_ref, ktf_ref, vf_ref, gcf_ref, grf_ref, hf_ref)
    bwd_refs = (qb_ref, kb_ref, qtb_ref, ktb_ref, vb_ref, gcb_ref, grb_ref, hb_ref)
    streams = []
    for bb in range(hf_ref.shape[0]):
        streams.append(tuple(r.at[bb:bb + 1] for r in fwd_refs) + (src_i <= tgt_i, 0, t - 1))
        streams.append(tuple(r.at[bb:bb + 1] for r in bwd_refs) + (src_i >= tgt_i, 2 * heads, 0))
    dn_nt = (((1,), (1,)), ((), ()))
    dn_tn = (((0,), (0,)), ((), ()))
    half = heads // 2

    def block_diag(a2):
        zero = jnp.zeros_like(a2)
        return jnp.concatenate([jnp.where(first_head, a2, zero), jnp.where(first_head, zero, a2)], axis=0)

    every = [(di, h, di * heads + h) for di in range(len(streams)) for h in range(heads)]
    pairs = [(di, p, di * half + p) for di in range(len(streams)) for p in range(half)]
    st = {}
    for di, h, idx in every:
        gc_ref, gr_ref, base, end = streams[di][5], streams[di][6], streams[di][9], streams[di][10]
        ig_r = gr_ref[0, base + h:base + h + 1, :]
        b_r = gr_ref[0, base + heads + h:base + heads + h + 1, :]
        total = b_r[:, end:end + 1]
        m_prev = m_ref[idx]
        st[idx] = dict(b_r=b_r, inter=b_r + m_prev, gk=total - b_r + ig_r, carry=total + m_prev,
                       src_c=gc_ref[0, :, base + h:base + h + 1] - gc_ref[0, :, base + heads + h:base + heads + h + 1])
    for di, h, idx in every:
        e = st[idx]
        m_new = jnp.maximum(e["carry"], jnp.max(e["gk"], axis=-1, keepdims=True))
        m_ref[idx] = m_new
        e["wk"] = jnp.exp(e["gk"] - m_new)
        e["decay"] = jnp.exp(e["carry"] - m_new)

    sp = {}
    for di, p, pi in pairs:
        sl = slice(2 * p * dqk, (2 * p + 2) * dqk)
        sp[pi] = dict(k2=streams[di][1][0, :, sl], n_prev=n_ref[pi])
        lhs = jnp.concatenate([sp[pi]["k2"], sp[pi]["n_prev"].astype(BF16)], axis=0)
        s2 = lax.dot_general(lhs, block_diag(streams[di][0][0, :, sl]), dn_nt, preferred_element_type=F32)
        for j in range(2):
            e = st[di * heads + 2 * p + j]
            e["qk"] = s2[:t, j * t:(j + 1) * t]
            e["qn"] = s2[t + j:t + j + 1, j * t:(j + 1) * t]

    for di, h, idx in every:
        e = st[idx]
        e["v"] = streams[di][4][0, :, h * dv:(h + 1) * dv]
        e["kwt"] = (streams[di][3][0, h * dqk:(h + 1) * dqk, :].astype(F32) * e["wk"]).astype(BF16)
    for di, h, idx in every:
        e = st[idx]
        c_new = e["decay"] * c_ref[idx] + jnp.dot(e["kwt"], e["v"], preferred_element_type=F32)
        c_ref[idx] = c_new
        rhs_ref[1 - par, idx, t:, :] = c_new.astype(BF16)
        rhs_ref[par, idx, :t, :] = e["v"]
    for di, p, pi in pairs:
        ea, eb = st[di * heads + 2 * p], st[di * heads + 2 * p + 1]
        wk2 = jnp.concatenate([ea["wk"], eb["wk"]], axis=1)
        w_rows = jnp.where(own_block, wk2, 0.0).astype(BF16)
        dec = jnp.where(n_row_q == 0, ea["decay"], eb["decay"])
        n_ref[pi] = dec * sp[pi]["n_prev"] + jnp.dot(w_rows, block_diag(sp[pi]["k2"]), preferred_element_type=F32)

    for di, h, idx in every:
        e = st[idx]
        e["dlog"] = jnp.where(streams[di][8], e["src_c"] + e["b_r"], -jnp.inf)
    for di, h, idx in every:
        e = st[idx]
        e["m_t"] = jnp.maximum(e["inter"], jnp.max(e["dlog"], axis=0, keepdims=True))
    for di, h, idx in every:
        e = st[idx]
        e["s"] = e["qk"] * jnp.exp(e["dlog"] - e["m_t"])
        e["sc"] = jnp.exp(e["inter"] - e["m_t"])
    for di, h, idx in every:
        e = st[idx]
        den = jnp.sum(e["s"], axis=0, keepdims=True) + e["sc"] * e["qn"]
        e["r"] = 1.0 / jnp.maximum(jnp.abs(den), jnp.exp(-e["m_t"]))
    for di, h, idx in every:
        e = st[idx]
        qt32 = streams[di][2][0, h * dqk:(h + 1) * dqk, :].astype(F32)
        e["lhs_t"] = jnp.concatenate(
            [(e["s"] * e["r"]).astype(BF16), (qt32 * (e["sc"] * e["r"])).astype(BF16)], axis=0)
    for di, h, idx in every:
        streams[di][7][0, :, h * dv:(h + 1) * dv] = lax.dot_general(
            st[idx]["lhs_t"], rhs_ref[par, idx], dn_tn, preferred_element_type=F32).astype(streams[di][7].dtype)


def _mlstm_scan(q, k, qt, kt, v, gc, gr, *, heads):
    b, l, mq = q.shape
    mv = v.shape[-1]
    dqk, dv = mq // heads, mv // heads
    t = min(SCAN_CHUNK, l)
    assert l % t == 0 and heads % 2 == 0
    nc = l // t
    ng = gr.shape[1]
    kern = functools.partial(_scan_kernel, t=t, heads=heads, dqk=dqk, dv=dv)
    fwd = lambda bi, c: (bi, c, 0)
    bwd = lambda bi, c: (bi, nc - 1 - c, 0)
    fwd_r = lambda bi, c: (bi, 0, c)
    bwd_r = lambda bi, c: (bi, 0, nc - 1 - c)

    rows = SCAN_ROWS if b % SCAN_ROWS == 0 else 1
    chains = 2 * rows * heads

    def stream(tok, rowm):
        return [pl.BlockSpec((rows, t, mq), tok), pl.BlockSpec((rows, t, mq), tok),
                pl.BlockSpec((rows, mq, t), rowm), pl.BlockSpec((rows, mq, t), rowm),
                pl.BlockSpec((rows, t, mv), tok),
                pl.BlockSpec((rows, t, V7X_LANES), tok), pl.BlockSpec((rows, ng, t), rowm)]

    return pl.pallas_call(
        kern,
        grid=(b // rows, nc),
        in_specs=stream(fwd, fwd_r) + stream(bwd, bwd_r),
        out_specs=[pl.BlockSpec((rows, t, mv), fwd), pl.BlockSpec((rows, t, mv), bwd)],
        out_shape=[jax.ShapeDtypeStruct((b, l, mv), BF16), jax.ShapeDtypeStruct((b, l, mv), BF16)],
        scratch_shapes=[pltpu.VMEM((chains, dqk, dv), F32), pltpu.VMEM((2, chains, t + dqk, dv), BF16),
                        pltpu.VMEM((chains // 2, N_ROWS, 2 * dqk), F32), pltpu.VMEM((chains, 1, 1), F32)],
        compiler_params=pltpu.CompilerParams(dimension_semantics=("parallel", "arbitrary")),
        name="mlstm_scan",
    )(q, k, qt, kt, v, gc, gr, q, k, qt, kt, v, gc, gr)


def _mout_kernel(hf_ref, hb_ref, x_ref, g_ref, wo_ref, hg_ref, w_ref, out_ref, y_ref, *, heads, dv):
    tm = x_ref.shape[1]
    parts = 4 if tm % (4 * V7X_MXU_DIM) == 0 else 1
    halves = [slice(r, r + tm // parts) for r in range(0, tm, tm // parts)]
    gate = [jnp.dot(_rmsnorm(x_ref[0, rows, :], g_ref[...]).astype(BF16), wo_ref[...], preferred_element_type=F32)
            for rows in halves]
    for rows, o in zip(halves, gate):
        for h in range(heads):
            sl = slice(h * dv, (h + 1) * dv)
            hs = hf_ref[0, rows, sl].astype(F32) + hb_ref[0, rows, sl].astype(F32)
            hs = hs * lax.rsqrt(jnp.mean(hs * hs, axis=-1, keepdims=True) + EPS)
            hs = hs * hg_ref[:, sl]
            y_ref[rows, sl] = (hs * jax.nn.sigmoid(o[:, sl])).astype(BF16)
        out_ref[0, rows, :] = x_ref[0, rows, :] + jnp.dot(y_ref[rows, :], w_ref[...], preferred_element_type=F32)


def _mlstm_out(hf, hb, x, norm_g, w_gate_out, head_g, w_out):
    b, l, d = x.shape
    heads, dv = head_g.shape
    mv = heads * dv
    tm = min(TOKEN_TILE, l)
    assert l % tm == 0
    row = lambda bi, i: (bi, i, 0)
    w = w_out.astype(BF16)
    wo = w_gate_out.astype(BF16)
    vmem = (w.size + wo.size) * 2 + 2 * tm * (2 * mv * 2 + 2 * d * 4) + tm * mv * 2 + 8 * tm * d * 4
    return pl.pallas_call(
        functools.partial(_mout_kernel, heads=heads, dv=dv),
        grid=(b, l // tm),
        in_specs=[pl.BlockSpec((1, tm, mv), row), pl.BlockSpec((1, tm, mv), row), pl.BlockSpec((1, tm, d), row),
                  _resident((1, d)), _resident(wo.shape), _resident((1, mv)), _resident(w.shape)],
        out_specs=pl.BlockSpec((1, tm, d), row),
        out_shape=jax.ShapeDtypeStruct((b, l, d), F32),
        scratch_shapes=[pltpu.VMEM((tm, mv), BF16)],
        compiler_params=pltpu.CompilerParams(
            dimension_semantics=("parallel", "parallel"), vmem_limit_bytes=_vmem_limit(vmem)),
        name="mlstm_out",
    )(hf, hb, x, norm_g.reshape(1, d), wo, head_g.reshape(1, mv), w)


def _dft_tables(l1, l2, gd, n2h, n2b):
    l = l1 * l2
    a1 = 2.0 * np.pi * np.outer(np.arange(l1), np.arange(l1)) / l1
    wa = np.stack([np.cos(a1), -np.sin(a1)]) / np.sqrt(l1)
    wa = np.einsum("rkn,ij->krinj", wa, np.eye(n2h)).reshape(l1 * 2 * n2h, l1 * n2h)
    tw = 2.0 * np.pi * np.einsum("k,jhi->jhki", np.arange(l1),
                                 np.arange(l2).reshape(l2 // n2b, n2b // n2h, n2h)) / l
    twr, twi = np.cos(tw)[..., None], -np.sin(tw)[..., None]
    a2 = 2.0 * np.pi * np.outer(np.arange(l2), np.arange(l2)) / l2
    c2, s2 = np.cos(a2), np.sin(a2)
    wc = np.block([[c2, s2], [-s2, c2]]) / np.sqrt(l2)
    ac = 2.0 * np.pi * np.outer(np.arange(gd), np.arange(gd)) / gd
    cs = np.stack([np.cos(ac), np.sin(ac)]) / np.sqrt(gd)
    return (jnp.asarray(wa, BF16), jnp.asarray(twr, F32), jnp.asarray(twi, F32),
            jnp.asarray(wc, BF16), jnp.asarray(cs, BF16))


def _fa_kernel(x_ref, g_ref, wa_ref, twr_ref, twi_ref, out_ref, *, l1, n2h, n2b, d):
    g = g_ref[...]
    wa = wa_ref[...]
    halves = []
    for hh in range(n2b // n2h):
        x = x_ref[0, :, hh * n2h:(hh + 1) * n2h, :].reshape(l1 * n2h, d)
        a = jnp.dot(wa, _rmsnorm(x, g).astype(BF16), preferred_element_type=F32)
        a = a.reshape(l1, 2, n2h, d)
        ar = a[:, 0]
        ai = a[:, 1]
        tr = twr_ref[0, hh]
        ti = twi_ref[0, hh]
        halves.append(jnp.stack([ar * tr - ai * ti, ar * ti + ai * tr], axis=1))
    out_ref[0] = jnp.concatenate(halves, axis=2).astype(BF16)


def _fc_kernel(ba_ref, x_ref, wc_ref, cs_ref, wout_ref, bout_ref, out_ref, z_ref, f_ref, *, l2, k1b, d, gd):
    wc = wc_ref[...]
    for j in range(k1b):
        z = jnp.dot(wc, ba_ref[0, j], preferred_element_type=F32)
        z_ref[0, j * l2:(j + 1) * l2, :] = z[:l2].astype(BF16)
        z_ref[1, j * l2:(j + 1) * l2, :] = z[l2:].astype(BF16)
    for gi in range(d // gd):
        sl = slice(gi * gd, (gi + 1) * gd)
        f = (jnp.dot(z_ref[0, :, sl], cs_ref[0], preferred_element_type=F32)
             + jnp.dot(z_ref[1, :, sl], cs_ref[1], preferred_element_type=F32))
        f_ref[:, sl] = f.astype(BF16)
    y = jnp.dot(f_ref[...], wout_ref[...], preferred_element_type=F32) + bout_ref[...]
    out_ref[0] = x_ref[0] + pltpu.einshape("jkd->kjd", y.reshape(k1b, l2, d))


def _fourier_mixer(x, norm_g, w_out, b_out):
    b, l, d = x.shape
    l2 = min(DFT_INNER, l)
    l1 = l // l2
    assert l1 * l2 == l and d % F_GROUPS == 0
    gd = d // F_GROUPS
    n2h = V7X_SUBLANES
    n2b = min(V7X_BF16_ROWS, l2)
    k1b = min(V7X_SUBLANES, l1)
    assert l2 % n2b == 0 and n2b % n2h == 0 and l1 % k1b == 0
    wa, twr, twi, wc, cs = _dft_tables(l1, l2, gd, n2h, n2b)

    tw_spec = pl.BlockSpec((1, n2b // n2h, l1, n2h, 1), lambda bi, j: (j, 0, 0, 0, 0))
    ba = pl.pallas_call(
        functools.partial(_fa_kernel, l1=l1, n2h=n2h, n2b=n2b, d=d),
        grid=(b, l2 // n2b),
        in_specs=[pl.BlockSpec((1, l1, n2b, d), lambda bi, j: (bi, 0, j, 0)),
                  _resident((1, d)), _resident(wa.shape), tw_spec, tw_spec],
        out_specs=pl.BlockSpec((1, l1, 2, n2b, d), lambda bi, j: (bi, 0, 0, j, 0)),
        out_shape=jax.ShapeDtypeStruct((b, l1, 2, l2, d), BF16),
        compiler_params=pltpu.CompilerParams(
            dimension_semantics=("parallel", "parallel"),
            vmem_limit_bytes=_vmem_limit(l1 * n2b * d * (2 * 4 + 2 * 2 * 2 + 6 * 4) + wa.size * 2)),
        name="fourier_seq_a",
    )(x.reshape(b, l1, l2, d), norm_g.reshape(1, d), wa, twr, twi)

    rows = k1b * l2
    out = pl.pallas_call(
        functools.partial(_fc_kernel, l2=l2, k1b=k1b, d=d, gd=gd),
        grid=(b, l1 // k1b),
        in_specs=[pl.BlockSpec((1, k1b, 2 * l2, d), lambda bi, j: (bi, j, 0, 0)),
                  pl.BlockSpec((1, l2, k1b, d), lambda bi, j: (bi, 0, j, 0)),
                  _resident(wc.shape), _resident(cs.shape), _resident((d, d)), _resident((1, d))],
        out_specs=pl.BlockSpec((1, l2, k1b, d), lambda bi, j: (bi, 0, j, 0)),
        out_shape=jax.ShapeDtypeStruct((b, l2, l1, d), F32),
        scratch_shapes=[pltpu.VMEM((2, rows, d), BF16), pltpu.VMEM((rows, d), BF16)],
        compiler_params=pltpu.CompilerParams(
            dimension_semantics=("parallel", "parallel"),
            vmem_limit_bytes=_vmem_limit(rows * d * (2 * 2 * 2 + 4 * 4 + 3 * 2 + 4 * 4) + 4 * d * d)),
        name="fourier_seq_c",
    )(ba.reshape(b, l1, 2 * l2, d), x.reshape(b, l2, l1, d), wc, cs, w_out.astype(BF16), b_out.reshape(1, d))
    return out.reshape(b, l, d)


def _mlstm_mixer(x, norm_g, w_in, b_gate, head_g, w_out):
    heads, dv = head_g.shape
    mv = heads * dv
    q, k, qt, kt, v, gc, gr = _mlstm_proj(x, norm_g, w_in, b_gate, heads=heads, mv=mv)
    hf, hb = _mlstm_scan(q, k, qt, kt, v, gc, gr, heads=heads)
    o_lo = 2 * q.shape[-1] + mv
    return _mlstm_out(hf, hb, x, norm_g, w_in[:, o_lo:o_lo + mv], head_g, w_out)


def _trunk(x, mix_norm_g, m_w_in, m_b_gate, m_head_g, m_w_out, f_w_out, f_b_out,
           ffn_norm_g, ffn_w_up, ffn_conv_w, ffn_conv_b, ffn_w_down, final_norm_g):
    depth = mix_norm_g.shape[0]
    n_mixers = 2
    for i in range(depth):
        j = i // n_mixers
        if i % n_mixers == 0:
            x = _mlstm_mixer(x, mix_norm_g[i], m_w_in[j], m_b_gate[j], m_head_g[j], m_w_out[j])
        else:
            x = _fourier_mixer(x, mix_norm_g[i], f_w_out[j], f_b_out[j])
        x = _conv_ffn(x, ffn_norm_g[i], ffn_w_up[i], ffn_conv_w[i], ffn_conv_b[i], ffn_w_down[i],
                      final_norm_g, final_norm=(i == depth - 1))
    return x


def kernel(x_prompt, x_sample, mix_norm_g, m_w_in, m_b_gate, m_head_g, m_w_out, f_w_out, f_b_out,
           ffn_norm_g, ffn_w_up, ffn_conv_w, ffn_conv_b, ffn_w_down, final_norm_g):
    params = (mix_norm_g, m_w_in, m_b_gate, m_head_g, m_w_out, f_w_out, f_b_out,
              ffn_norm_g, ffn_w_up, ffn_conv_w, ffn_conv_b, ffn_w_down, final_norm_g)
    return (_trunk(x_prompt, *params), _trunk(x_sample, *params))
```
